```python
import math
import jax, jax.numpy as jnp
from jax import lax
import numpy as np

D_MODEL = 2048
BATCH = 1
SEQ = 16384
DEPTH = 1

GRID_W = 64
ROPE_THETA = 10000.0
N_Q_HEADS = 8
N_KV_HEADS = 2
HEAD_DIM = 128
GQA_GROUP = N_Q_HEADS // N_KV_HEADS
BLOCK_Q = 128
ATTN_WIDTH = N_Q_HEADS * HEAD_DIM
KV_WIDTH = N_KV_HEADS * HEAD_DIM
SGU_GROUPS = 8
SGU_GROUP_DIM = 128
SGU_WIDTH = SGU_GROUPS * SGU_GROUP_DIM
SGU_CHUNK = 128
N_BRANCHES = 2
IN_WIDTH = ATTN_WIDTH + 2 * KV_WIDTH + 2 * SGU_WIDTH + N_BRANCHES * D_MODEL
SPLITS = (ATTN_WIDTH,
          ATTN_WIDTH + KV_WIDTH,
          ATTN_WIDTH + 2 * KV_WIDTH,
          ATTN_WIDTH + 2 * KV_WIDTH + SGU_WIDTH,
          ATTN_WIDTH + 2 * KV_WIDTH + 2 * SGU_WIDTH)
PEER_HEADS = 8
PEER_N_KEYS = 128
PEER_N_EXPERTS = PEER_N_KEYS * PEER_N_KEYS
PEER_QUERY_DIM = 256
PEER_HALF = PEER_QUERY_DIM // 2
PEER_TOPK = 16
PEER_BLOCK = 128
EPS = 1e-6

kernel_name = "hybrid_gqa_sgu_peer_block"


def rmsnorm(x, g):
    xf = x.astype(jnp.float32)
    y = xf * lax.rsqrt(jnp.mean(xf * xf, axis=-1, keepdims=True) + EPS)
    return (y * g.astype(jnp.float32)).astype(x.dtype)


def axial_rope_tables(seq_len):
    rows = seq_len // GRID_W
    row = jnp.repeat(jnp.arange(rows, dtype=jnp.float32), GRID_W)
    col = jnp.tile(jnp.arange(GRID_W, dtype=jnp.float32), rows)
    n_pairs = HEAD_DIM // 4
    inv_freq = ROPE_THETA ** (-jnp.arange(n_pairs, dtype=jnp.float32) / n_pairs)
    ang = jnp.concatenate([row[:, None] * inv_freq, col[:, None] * inv_freq], axis=-1)
    return jnp.cos(ang), jnp.sin(ang)


def apply_rope(x, cos, sin):
    B, S, H, D = x.shape
    xp = x.astype(jnp.float32).reshape(B, S, H, D // 2, 2)
    x1, x2 = xp[..., 0], xp[..., 1]
    c = cos[None, :, None, :]
    s = sin[None, :, None, :]
    out = jnp.stack([x1 * c - x2 * s, x1 * s + x2 * c], axis=-1)
    return out.reshape(B, S, H, D).astype(x.dtype)


def block_attention(q, k, v):
    B, S, _, hd = q.shape
    n_blk = S // BLOCK_Q
    scale = 1.0 / math.sqrt(HEAD_DIM)
    qb = jnp.moveaxis(q.reshape(B, n_blk, BLOCK_Q, N_KV_HEADS, GQA_GROUP, hd), 1, 0)

    def one_block(q_blk):
        s = jnp.einsum('bqkgd,bskd->bkgqs', q_blk, k).astype(jnp.float32) * scale
        p = jax.nn.softmax(s, axis=-1).astype(v.dtype)
        return jnp.einsum('bkgqs,bskd->bqkgd', p, v)

    o = lax.map(one_block, qb)
    return jnp.moveaxis(o, 0, 1).reshape(B, S, N_Q_HEADS * hd)


def spatial_gating(u, v, g_v, w_s, b_s):
    B, S, _ = u.shape
    n_chunk = S // SGU_CHUNK
    vn = rmsnorm(v, g_v).reshape(B, n_chunk, SGU_CHUNK, SGU_GROUPS, SGU_GROUP_DIM)
    mixed = jnp.einsum('gpq,bcqgd->bcpgd', w_s, vn) + b_s.T[None, None, :, :, None]
    return u * mixed.reshape(B, S, SGU_WIDTH)


def peer_ffn(xn, w_q, sub_k1, sub_k2, u_tab, v_tab):
    B, S, D = xn.shape
    q = (xn @ w_q).reshape(B, S, PEER_HEADS, PEER_QUERY_DIM)
    s1 = jnp.einsum('bshd,hnd->bshn', q[..., :PEER_HALF], sub_k1).astype(jnp.float32)
    s2 = jnp.einsum('bshd,hnd->bshn', q[..., PEER_HALF:], sub_k2).astype(jnp.float32)
    v1, i1 = lax.top_k(s1, PEER_TOPK)
    v2, i2 = lax.top_k(s2, PEER_TOPK)
    cand = (v1[..., :, None] + v2[..., None, :]).reshape(B, S, PEER_HEADS, PEER_TOPK * PEER_TOPK)
    cv, ci = lax.top_k(cand, PEER_TOPK)
    e1 = jnp.take_along_axis(i1, ci // PEER_TOPK, axis=-1)
    e2 = jnp.take_along_axis(i2, ci % PEER_TOPK, axis=-1)
    experts = (e1 * PEER_N_KEYS + e2).astype(jnp.int32)
    gates = jax.nn.softmax(cv, axis=-1).astype(xn.dtype)
    T = B * S
    n_blk = T // PEER_BLOCK
    hk = PEER_HEADS * PEER_TOPK
    xt = xn.reshape(n_blk, PEER_BLOCK, D)
    et = experts.reshape(n_blk, PEER_BLOCK, hk)
    gt = gates.reshape(n_blk, PEER_BLOCK, hk)

    def one_block(args):
        xb, eb, gb = args
        ue = jnp.take(u_tab, eb, axis=0)
        a = jax.nn.gelu(jnp.einsum('tkd,td->tk', ue, xb)) * gb
        ve = jnp.take(v_tab, eb, axis=0)
        return jnp.einsum('tk,tkd->td', a, ve)

    return lax.map(one_block, (xt, et, gt)).reshape(B, S, D)


def setup_inputs(seed: int = 0) -> dict:
    key = jax.random.key(seed)
    ks = jax.random.split(key, 20)
    f32 = jnp.float32
    nrm = lambda k, shape, scale: jax.random.normal(k, shape, f32) * scale
    L = DEPTH
    return {
        "x": nrm(ks[0], (BATCH, SEQ, D_MODEL), 1.0),
        "norm_mix": 1.0 + nrm(ks[1], (L, D_MODEL), 0.01),
        "w_in": nrm(ks[2], (L, D_MODEL, IN_WIDTH), D_MODEL ** -0.5),
        "b_gate": nrm(ks[3], (L, N_BRANCHES * D_MODEL), 0.01),
        "q_norm": 1.0 + nrm(ks[4], (L, HEAD_DIM), 0.01),
        "k_norm": 1.0 + nrm(ks[5], (L, HEAD_DIM), 0.01),
        "sgu_norm": 1.0 + nrm(ks[6], (L, SGU_WIDTH), 0.01),
        "w_sgu": nrm(ks[7], (L, SGU_GROUPS, SGU_CHUNK, SGU_CHUNK), SGU_CHUNK ** -0.5),
        "b_sgu": 1.0 + nrm(ks[8], (L, SGU_GROUPS, SGU_CHUNK), 0.01),
        "w_attn_proj": nrm(ks[9], (L, ATTN_WIDTH, D_MODEL), ATTN_WIDTH ** -0.5),
        "w_sgu_proj": nrm(ks[10], (L, SGU_WIDTH, D_MODEL), SGU_WIDTH ** -0.5),
        "w_out": nrm(ks[11], (L, D_MODEL, D_MODEL), D_MODEL ** -0.5),
        "norm_ffn": 1.0 + nrm(ks[12], (L, D_MODEL), 0.01),
        "w_peer_q": nrm(ks[13], (L, D_MODEL, PEER_HEADS * PEER_QUERY_DIM), D_MODEL ** -0.5),
        "peer_k1": nrm(ks[14], (L, PEER_HEADS, PEER_N_KEYS, PEER_HALF), PEER_HALF ** -0.5),
        "peer_k2": nrm(ks[15], (L, PEER_HEADS, PEER_N_KEYS, PEER_HALF), PEER_HALF ** -0.5),
        "peer_u": nrm(ks[16], (L, PEER_N_EXPERTS, D_MODEL), D_MODEL ** -0.5),
        "peer_v": nrm(ks[17], (L, PEER_N_EXPERTS, D_MODEL), PEER_HEADS ** -0.5),
    }


def reference(x, norm_mix, w_in, b_gate, q_norm, k_norm, sgu_norm, w_sgu, b_sgu,
              w_attn_proj, w_sgu_proj, w_out, norm_ffn, w_peer_q, peer_k1, peer_k2,
              peer_u, peer_v):
    B, S, _ = x.shape
    cos, sin = axial_rope_tables(S)
    h = x
    for l in range(DEPTH):
        hn = rmsnorm(h, norm_mix[l])
        z = hn @ w_in[l]
        q, k, v, su, sv, g = jnp.split(z, SPLITS, axis=-1)
        q = apply_rope(rmsnorm(q.reshape(B, S, N_Q_HEADS, HEAD_DIM), q_norm[l]), cos, sin)
        k = apply_rope(rmsnorm(k.reshape(B, S, N_KV_HEADS, HEAD_DIM), k_norm[l]), cos, sin)
        v = v.reshape(B, S, N_KV_HEADS, HEAD_DIM)
        attn = block_attention(q, k, v)
        sgu = spatial_gating(jax.nn.gelu(su), jax.nn.gelu(sv), sgu_norm[l], w_sgu[l], b_sgu[l])
        gates = jax.nn.sigmoid((g + b_gate[l]).astype(jnp.float32)).astype(h.dtype)
        g_attn, g_sgu = jnp.split(gates, N_BRANCHES, axis=-1)
        merged = g_attn * (attn @ w_attn_proj[l]) + g_sgu * (sgu @ w_sgu_proj[l])
        h = h + merged @ w_out[l]
        h = h + peer_ffn(rmsnorm(h, norm_ffn[l]), w_peer_q[l], peer_k1[l], peer_k2[l],
                         peer_u[l], peer_v[l])
    return h
```

```python
import functools
import math

import jax
import jax.numpy as jnp
import numpy as np
from jax import lax
from jax.experimental import pallas as pl
from jax.experimental.pallas import tpu as pltpu

F32 = jnp.float32
BF16 = jnp.bfloat16

D_MODEL = 2048
GRID_W = 64
ROPE_THETA = 10000.0
N_Q_HEADS = 8
N_KV_HEADS = 2
HEAD_DIM = 128
GQA_GROUP = N_Q_HEADS // N_KV_HEADS
ATTN_WIDTH = N_Q_HEADS * HEAD_DIM
KV_WIDTH = N_KV_HEADS * HEAD_DIM
SGU_GROUPS = 8
SGU_CHUNK = 128
SGU_WIDTH = SGU_GROUPS * 128
PEER_HEADS = 8
PEER_N_KEYS = 128
PEER_N_EXPERTS = PEER_N_KEYS * PEER_N_KEYS
PEER_QUERY_DIM = 256
PEER_HALF = PEER_QUERY_DIM // 2
PEER_TOPK = 16
EPS = 1e-6

LANES = 128
SUBLANES = 8
VMEM_LIMIT = 56 * 1024 * 1024

N_RANK = PEER_TOPK + 1
RANK_ROWS = 24


def _gelu(x):
    c = math.sqrt(2.0 / math.pi)
    return 0.5 * x * (1.0 + jnp.tanh(c * (x + 0.044715 * (x * x * x))))


def _sigmoid(x):
    return 1.0 / (1.0 + jnp.exp(-x))


def _params(sem, vmem=VMEM_LIMIT):
    return pltpu.CompilerParams(dimension_semantics=sem, vmem_limit_bytes=vmem)


def _rmsnorm_kernel(x_ref, g_ref, o_ref):
    x = x_ref[...]
    ms = jnp.mean(x * x, axis=-1, keepdims=True)
    o_ref[...] = (x * lax.rsqrt(ms + EPS) * g_ref[...]).astype(o_ref.dtype)


def _rmsnorm(x, g, tm=512):
    s, d = x.shape
    return pl.pallas_call(
        _rmsnorm_kernel,
        grid=(s // tm,),
        in_specs=[pl.BlockSpec((tm, d), lambda i: (i, 0)),
                  pl.BlockSpec((1, d), lambda i: (0, 0))],
        out_specs=pl.BlockSpec((tm, d), lambda i: (i, 0)),
        out_shape=jax.ShapeDtypeStruct((s, d), BF16),
        compiler_params=_params(("parallel",)),
        name="rmsnorm",
    )(x, g)


def _qkv_kernel(hn_ref, w_ref, qg_ref, kg_ref, c_ref, s_ref, q_ref, k_ref, v_ref):
    z = jnp.dot(hn_ref[...], w_ref[...], preferred_element_type=F32)
    c = c_ref[...]
    s = s_ref[...]

    def norm_rope(xh, g):
        ms = jnp.mean(xh * xh, axis=-1, keepdims=True)
        y = xh * lax.rsqrt(ms + EPS) * g
        return y * c + pltpu.roll(y, HEAD_DIM // 2, axis=1) * s

    scale = 1.0 / math.sqrt(HEAD_DIM)
    for h in range(N_Q_HEADS):
        sl = slice(h * HEAD_DIM, (h + 1) * HEAD_DIM)
        q_ref[:, sl] = (norm_rope(z[:, sl], qg_ref[...]) * scale).astype(q_ref.dtype)
    for h in range(N_KV_HEADS):
        sl = slice(h * HEAD_DIM, (h + 1) * HEAD_DIM)
        zs = slice(ATTN_WIDTH + h * HEAD_DIM, ATTN_WIDTH + (h + 1) * HEAD_DIM)
        k_ref[:, sl] = norm_rope(z[:, zs], kg_ref[...]).astype(k_ref.dtype)
    v_ref[...] = z[:, ATTN_WIDTH + KV_WIDTH:].astype(v_ref.dtype)


def _qkv(hn, w_qkv, qg, kg, cos2, sin2, tm=512):
    s, d = hn.shape
    n = w_qkv.shape[1]
    return pl.pallas_call(
        _qkv_kernel,
        grid=(s // tm,),
        in_specs=[pl.BlockSpec((tm, d), lambda i: (i, 0)),
                  pl.BlockSpec((d, n), lambda i: (0, 0)),
                  pl.BlockSpec((1, HEAD_DIM), lambda i: (0, 0)),
                  pl.BlockSpec((1, HEAD_DIM), lambda i: (0, 0)),
                  pl.BlockSpec((tm, HEAD_DIM), lambda i: (i, 0)),
                  pl.BlockSpec((tm, HEAD_DIM), lambda i: (i, 0))],
        out_specs=[pl.BlockSpec((tm, ATTN_WIDTH), lambda i: (i, 0)),
                   pl.BlockSpec((tm, KV_WIDTH), lambda i: (i, 0)),
                   pl.BlockSpec((tm, KV_WIDTH), lambda i: (i, 0))],
        out_shape=[jax.ShapeDtypeStruct((s, ATTN_WIDTH), BF16),
                   jax.ShapeDtypeStruct((s, KV_WIDTH), BF16),
                   jax.ShapeDtypeStruct((s, KV_WIDTH), BF16)],
        compiler_params=_params(("parallel",)),
        name="qkv_rope",
    )(hn, w_qkv, qg, kg, cos2, sin2)


def _attn_kernel(q_ref, k_ref, v_ref, o_ref, m_ref, l_ref, acc_ref):
    ki = pl.program_id(2)
    tk = k_ref.shape[0]

    @pl.when(ki == 0)
    def _():
        m_ref[...] = jnp.full(m_ref.shape, -jnp.inf, F32)
        l_ref[...] = jnp.zeros(l_ref.shape, F32)
        acc_ref[...] = jnp.zeros(acc_ref.shape, F32)

    k = k_ref[...]
    v = v_ref[...]
    for g in range(GQA_GROUP):
        q = q_ref[:, g * HEAD_DIM:(g + 1) * HEAD_DIM]
        s = lax.dot_general(q, k, (((1,), (1,)), ((), ())), preferred_element_type=F32)
        m_prev = m_ref[g]
        m_new = jnp.maximum(m_prev, jnp.max(s, axis=1, keepdims=True))
        alpha = jnp.exp(m_prev - m_new)
        p = jnp.exp(s - jnp.tile(m_new, (1, tk // LANES)))
        l_ref[g] = alpha * l_ref[g] + jnp.sum(p, axis=1, keepdims=True)
        acc_ref[g] = alpha * acc_ref[g] + jnp.dot(p.astype(v.dtype), v, preferred_element_type=F32)
        m_ref[g] = m_new

    @pl.when(ki == pl.num_programs(2) - 1)
    def _():
        for g in range(GQA_GROUP):
            o_ref[:, g * HEAD_DIM:(g + 1) * HEAD_DIM] = (acc_ref[g] / l_ref[g]).astype(o_ref.dtype)


def _attention(q, k, v, tq=512, tk=2048):
    s = q.shape[0]
    gw = GQA_GROUP * HEAD_DIM
    return pl.pallas_call(
        _attn_kernel,
        grid=(N_KV_HEADS, s // tq, s // tk),
        in_specs=[pl.BlockSpec((tq, gw), lambda h, i, j: (i, h)),
                  pl.BlockSpec((tk, HEAD_DIM), lambda h, i, j: (j, h)),
                  pl.BlockSpec((tk, HEAD_DIM), lambda h, i, j: (j, h))],
        out_specs=pl.BlockSpec((tq, gw), lambda h, i, j: (i, h)),
        out_shape=jax.ShapeDtypeStruct((s, ATTN_WIDTH), BF16),
        scratch_shapes=[pltpu.VMEM((GQA_GROUP, tq, LANES), F32),
                        pltpu.VMEM((GQA_GROUP, tq, LANES), F32),
                        pltpu.VMEM((GQA_GROUP, tq, HEAD_DIM), F32)],
        compiler_params=_params(("parallel", "parallel", "arbitrary")),
        name="flash_attention",
    )(q, k, v)


def _sgu_kernel(hn_ref, wu_ref, wv_ref, gn_ref, ws_ref, bs_ref, o_ref):
    hn = hn_ref[...]
    tm = hn.shape[0]
    u = _gelu(jnp.dot(hn, wu_ref[...], preferred_element_type=F32))
    v = _gelu(jnp.dot(hn, wv_ref[...], preferred_element_type=F32))
    ms = jnp.mean(v * v, axis=-1, keepdims=True)
    vn = (v * lax.rsqrt(ms + EPS) * gn_ref[...]).astype(BF16)
    for c in range(tm // SGU_CHUNK):
        rows = slice(c * SGU_CHUNK, (c + 1) * SGU_CHUNK)
        for g in range(SGU_GROUPS):
            cols = slice(g * 128, (g + 1) * 128)
            mixed = jnp.dot(ws_ref[g], vn[rows, cols], preferred_element_type=F32) + bs_ref[g]
            o_ref[rows, cols] = (u[rows, cols] * mixed).astype(o_ref.dtype)


def _sgu(hn, wu, wv, gn, ws, bs, tm=256):
    s, d = hn.shape
    return pl.pallas_call(
        _sgu_kernel,
        grid=(s // tm,),
        in_specs=[pl.BlockSpec((tm, d), lambda i: (i, 0)),
                  pl.BlockSpec((d, SGU_WIDTH), lambda i: (0, 0)),
                  pl.BlockSpec((d, SGU_WIDTH), lambda i: (0, 0)),
                  pl.BlockSpec((1, SGU_WIDTH), lambda i: (0, 0)),
                  pl.BlockSpec((SGU_GROUPS, SGU_CHUNK, SGU_CHUNK), lambda i: (0, 0, 0)),
                  pl.BlockSpec((SGU_GROUPS, SGU_CHUNK, 128), lambda i: (0, 0, 0))],
        out_specs=pl.BlockSpec((tm, SGU_WIDTH), lambda i: (i, 0)),
        out_shape=jax.ShapeDtypeStruct((s, SGU_WIDTH), BF16),
        compiler_params=_params(("parallel",)),
        name="sgu",
    )(hn, wu, wv, gn, ws, bs)


def _merge_kernel(hn_ref, a_ref, s_ref, wga_ref, wgs_ref, ba_ref, bs_ref, pa_ref, pg_ref, o_ref):
    hn = hn_ref[...]
    ga = _sigmoid(jnp.dot(hn, wga_ref[...], preferred_element_type=F32) + ba_ref[...])
    gs = _sigmoid(jnp.dot(hn, wgs_ref[...], preferred_element_type=F32) + bs_ref[...])
    ya = jnp.dot(a_ref[...], pa_ref[...], preferred_element_type=F32)
    ys = jnp.dot(s_ref[...], pg_ref[...], preferred_element_type=F32)
    o_ref[...] = (ga * ya + gs * ys).astype(o_ref.dtype)


def _merge(hn, attn, sgu, wga, wgs, ba, bs, pa, pg, tm=512, tn=512):
    s, d = hn.shape
    return pl.pallas_call(
        _merge_kernel,
        grid=(d // tn, s // tm),
        in_specs=[pl.BlockSpec((tm, d), lambda j, i: (i, 0)),
                  pl.BlockSpec((tm, ATTN_WIDTH), lambda j, i: (i, 0)),
                  pl.BlockSpec((tm, SGU_WIDTH), lambda j, i: (i, 0)),
                  pl.BlockSpec((d, tn), lambda j, i: (0, j)),
                  pl.BlockSpec((d, tn), lambda j, i: (0, j)),
                  pl.BlockSpec((1, tn), lambda j, i: (0, j)),
                  pl.BlockSpec((1, tn), lambda j, i: (0, j)),
                  pl.BlockSpec((ATTN_WIDTH, tn), lambda j, i: (0, j)),
                  pl.BlockSpec((SGU_WIDTH, tn), lambda j, i: (0, j))],
        out_specs=pl.BlockSpec((tm, tn), lambda j, i: (i, j)),
        out_shape=jax.ShapeDtypeStruct((s, d), BF16),
        compiler_params=_params(("parallel", "parallel")),
        name="gated_merge",
    )(hn, attn, sgu, wga, wgs, ba, bs, pa, pg)


def _out_kernel(x_ref, mg_ref, wo_ref, g_ref, h_ref, xn_ref):
    h = x_ref[...] + jnp.dot(mg_ref[...], wo_ref[...], preferred_element_type=F32)
    h_ref[...] = h
    ms = jnp.mean(h * h, axis=-1, keepdims=True)
    xn_ref[...] = (h * lax.rsqrt(ms + EPS) * g_ref[...]).astype(xn_ref.dtype)


def _out_proj(x, merged, wo, g, tm=256):
    s, d = x.shape
    return pl.pallas_call(
        _out_kernel,
        grid=(s // tm,),
        in_specs=[pl.BlockSpec((tm, d), lambda i: (i, 0)),
                  pl.BlockSpec((tm, d), lambda i: (i, 0)),
                  pl.BlockSpec((d, d), lambda i: (0, 0)),
                  pl.BlockSpec((1, d), lambda i: (0, 0))],
        out_specs=[pl.BlockSpec((tm, d), lambda i: (i, 0)),
                   pl.BlockSpec((tm, d), lambda i: (i, 0))],
        out_shape=[jax.ShapeDtypeStruct((s, d), F32),
                   jax.ShapeDtypeStruct((s, d), BF16)],
        compiler_params=_params(("parallel",)),
        name="out_proj_norm",
    )(x, merged, wo, g)


def _rank_rows(x, k, out_ref):
    rows = x.shape[0]
    iota = lax.broadcasted_iota(jnp.int32, x.shape, 0)
    for r in range(k):
        m = jnp.max(x, axis=0, keepdims=True)
        first = jnp.min(jnp.where(x == m, iota, rows), axis=0, keepdims=True)
        x = jnp.where(iota == first, -jnp.inf, x)
        out_ref[r:r + 1, :] = m


def _peer_route_kernel(xn_ref, wq_ref, k1_ref, k2_ref,
                       a1_ref, p1_ref, s2_ref, p2_ref,
                       st_scr, v1_scr, v2_scr, cand_scr, top_scr):
    pq = jnp.dot(xn_ref[...], wq_ref[...], preferred_element_type=F32).astype(BF16)
    nt = (((1,), (1,)), ((), ()))
    for h in range(PEER_HEADS):
        base = h * PEER_QUERY_DIM
        st_scr[2 * h] = lax.dot_general(k1_ref[h], pq[:, base:base + PEER_HALF], nt,
                                        preferred_element_type=F32)
        st_scr[2 * h + 1] = lax.dot_general(k2_ref[h], pq[:, base + PEER_HALF:base + PEER_QUERY_DIM], nt,
                                            preferred_element_type=F32)

    neg = jnp.full(v1_scr.shape, -jnp.inf, F32)
    v1_scr[...] = neg
    v2_scr[...] = neg

    def per_head(h, carry):
        s1 = st_scr[2 * h]
        s2 = st_scr[2 * h + 1]
        _rank_rows(s1, N_RANK, v1_scr)
        _rank_rows(s2, N_RANK, v2_scr)
        cand_scr[0:RANK_ROWS, :] = v1_scr[0:1, :] + v2_scr[...]
        for i in range(1, SUBLANES):
            lo = RANK_ROWS + (i - 1) * SUBLANES
            cand_scr[lo:lo + SUBLANES, :] = v1_scr[i:i + 1, :] + v2_scr[0:SUBLANES, :]
        lo = RANK_ROWS + (SUBLANES - 1) * SUBLANES
        cand_scr[lo:lo + 2 * SUBLANES, :] = v1_scr[SUBLANES:RANK_ROWS, :] + v2_scr[0:1, :]
        _rank_rows(cand_scr[...], N_RANK, top_scr)
        top = top_scr[...]
        best = top[0:1, :]
        cut = 0.5 * (top[PEER_TOPK - 1:PEER_TOPK, :] + top[PEER_TOPK:PEER_TOPK + 1, :])
        z = jnp.sum(jnp.exp(top[0:PEER_TOPK, :] - best), axis=0, keepdims=True)
        a1_ref[h] = cut - s1
        s2_ref[h] = s2
        p1_ref[h] = jnp.exp(s1 - v1_scr[0:1, :])
        p2_ref[h] = jnp.exp(s2 - v2_scr[0:1, :]) / z
        return carry

    lax.fori_loop(0, PEER_HEADS, per_head, 0)


def _peer_route(xn, wq, k1, k2, tm=256):
    s, d = xn.shape
    stat = jax.ShapeDtypeStruct((PEER_HEADS, PEER_N_KEYS, s), F32)
    stat_spec = pl.BlockSpec((PEER_HEADS, PEER_N_KEYS, tm), lambda i: (0, 0, i))
    cand_rows = RANK_ROWS + (SUBLANES - 1) * SUBLANES + 2 * SUBLANES
    return pl.pallas_call(
        _peer_route_kernel,
        grid=(s // tm,),
        in_specs=[pl.BlockSpec((tm, d), lambda i: (i, 0)),
                  pl.BlockSpec((d, PEER_HEADS * PEER_QUERY_DIM), lambda i: (0, 0)),
                  pl.BlockSpec((PEER_HEADS, PEER_N_KEYS, PEER_HALF), lambda i: (0, 0, 0)),
                  pl.BlockSpec((PEER_HEADS, PEER_N_KEYS, PEER_HALF), lambda i: (0, 0, 0))],
        out_specs=[stat_spec, stat_spec, stat_spec, stat_spec],
        out_shape=[stat, stat, stat, stat],
        scratch_shapes=[pltpu.VMEM((2 * PEER_HEADS, PEER_N_KEYS, tm), F32),
                        pltpu.VMEM((RANK_ROWS, tm), F32),
                        pltpu.VMEM((RANK_ROWS, tm), F32),
                        pltpu.VMEM((cand_rows, tm), F32),
                        pltpu.VMEM((RANK_ROWS, tm), F32)],
        compiler_params=_params(("parallel",)),
        name="peer_route",
    )(xn, wq, k1, k2)


def _peer_dense_kernel(xn_ref, h_ref, a1_ref, p1_ref, s2_ref, p2_ref, u_ref, v_ref,
                       o_ref, st_scr, at_scr):
    eb = pl.program_id(1)
    tt = xn_ref.shape[0]
    n_e1 = a1_ref.shape[1]

    @pl.when(eb == 0)
    def _():
        o_ref[...] = h_ref[...]

    st_scr[...] = lax.dot_general(u_ref[...], xn_ref[...], (((1,), (1,)), ((), ())),
                                  preferred_element_type=F32)
    for j in range(n_e1):
        rows = slice(j * PEER_N_KEYS, (j + 1) * PEER_N_KEYS)
        for lt in range(tt // LANES):
            cols = slice(lt * LANES, (lt + 1) * LANES)
            w = jnp.zeros((PEER_N_KEYS, LANES), F32)
            for h in range(PEER_HEADS):
                keep = s2_ref[h, :, cols] >= a1_ref[h, j:j + 1, cols]
                w = w + p1_ref[h, j:j + 1, cols] * jnp.where(keep, p2_ref[h, :, cols], 0.0)
            at_scr[rows, cols] = (_gelu(st_scr[rows, cols]) * w).astype(at_scr.dtype)
    o_ref[...] += lax.dot_general(at_scr[...], v_ref[...], (((0,), (0,)), ((), ())),
                                  preferred_element_type=F32)


def _peer_dense(xn, h1, a1, p1, s2, p2, u, v, tt=512, te=1024):
    s, d = xn.shape
    n_e1 = te // PEER_N_KEYS
    once = pl.Buffered(1)
    return pl.pallas_call(
        _peer_dense_kernel,
        grid=(s // tt, PEER_N_EXPERTS // te),
        in_specs=[pl.BlockSpec((tt, d), lambda t, e: (t, 0), pipeline_mode=once),
                  pl.BlockSpec((tt, d), lambda t, e: (t, 0), pipeline_mode=once),
                  pl.BlockSpec((PEER_HEADS, n_e1, tt), lambda t, e: (0, e, t)),
                  pl.BlockSpec((PEER_HEADS, n_e1, tt), lambda t, e: (0, e, t)),
                  pl.BlockSpec((PEER_HEADS, PEER_N_KEYS, tt), lambda t, e: (0, 0, t), pipeline_mode=once),
                  pl.BlockSpec((PEER_HEADS, PEER_N_KEYS, tt), lambda t, e: (0, 0, t), pipeline_mode=once),
                  pl.BlockSpec((te, d), lambda t, e: (e, 0)),
                  pl.BlockSpec((te, d), lambda t, e: (e, 0))],
        out_specs=pl.BlockSpec((tt, d), lambda t, e: (t, 0)),
        out_shape=jax.ShapeDtypeStruct((s, d), F32),
        scratch_shapes=[pltpu.VMEM((te, tt), F32),
                        pltpu.VMEM((te, tt), BF16)],
        compiler_params=_params(("parallel", "arbitrary")),
        name="peer_dense",
    )(xn, h1, a1, p1, s2, p2, u, v)


def _rope_tables(seq_len):
    rows = seq_len // GRID_W
    row = jnp.repeat(jnp.arange(rows, dtype=F32), GRID_W)
    col = jnp.tile(jnp.arange(GRID_W, dtype=F32), rows)
    n_pairs = HEAD_DIM // 4
    inv_freq = ROPE_THETA ** (-jnp.arange(n_pairs, dtype=F32) / n_pairs)
    ang = jnp.concatenate([row[:, None] * inv_freq, col[:, None] * inv_freq], axis=-1)
    cos, sin = jnp.cos(ang), jnp.sin(ang)
    return jnp.concatenate([cos, cos], axis=-1), jnp.concatenate([-sin, sin], axis=-1)


def kernel(x, norm_mix, w_in, b_gate, q_norm, k_norm, sgu_norm, w_sgu, b_sgu, w_attn_proj,
           w_sgu_proj, w_out, norm_ffn, w_peer_q, peer_k1, peer_k2, peer_u, peer_v):
    b, s, d = x.shape
    assert b == 1 and norm_mix.shape[0] == 1
    l = 0
    xs = x.reshape(s, d)

    pair_perm = np.concatenate([np.arange(0, HEAD_DIM, 2), np.arange(1, HEAD_DIM, 2)])
    q_cols = (np.arange(N_Q_HEADS)[:, None] * HEAD_DIM + pair_perm[None, :]).reshape(-1)
    k_cols = ATTN_WIDTH + (np.arange(N_KV_HEADS)[:, None] * HEAD_DIM + pair_perm[None, :]).reshape(-1)
    v_cols = ATTN_WIDTH + KV_WIDTH + np.arange(KV_WIDTH)
    w = w_in[l]
    o_su = ATTN_WIDTH + 2 * KV_WIDTH
    o_sv = o_su + SGU_WIDTH
    o_g = o_sv + SGU_WIDTH
    w_qkv = w[:, np.concatenate([q_cols, k_cols, v_cols])].astype(BF16)
    w_su = w[:, o_su:o_sv].astype(BF16)
    w_sv = w[:, o_sv:o_g].astype(BF16)
    w_ga = w[:, o_g:o_g + d].astype(BF16)
    w_gs = w[:, o_g + d:].astype(BF16)
    cos2, sin2 = _rope_tables(s)

    hn = _rmsnorm(xs, norm_mix[l].reshape(1, d))
    q, k, v = _qkv(hn, w_qkv, q_norm[l][pair_perm].reshape(1, HEAD_DIM),
                   k_norm[l][pair_perm].reshape(1, HEAD_DIM), cos2, sin2)
    attn = _attention(q, k, v)
    bias_s = jnp.broadcast_to(b_sgu[l][:, :, None], (SGU_GROUPS, SGU_CHUNK, 128))
    sgu = _sgu(hn, w_su, w_sv, sgu_norm[l].reshape(1, SGU_WIDTH), w_sgu[l].astype(BF16), bias_s)
    merged = _merge(hn, attn, sgu, w_ga, w_gs, b_gate[l][:d].reshape(1, d), b_gate[l][d:].reshape(1, d),
                    w_attn_proj[l].astype(BF16), w_sgu_proj[l].astype(BF16))
    h1, xn2 = _out_proj(xs, merged, w_out[l].astype(BF16), norm_ffn[l].reshape(1, d))
    a1, p1, s2, p2 = _peer_route(xn2, w_peer_q[l].astype(BF16), peer_k1[l].astype(BF16),
                                 peer_k2[l].astype(BF16))
    out = _peer_dense(xn2, h1, a1, p1, s2, p2, peer_u[l].astype(BF16), peer_v[l].astype(BF16))
    return out.reshape(b, s, d)
```

```python
import functools
import math

import jax
import jax.numpy as jnp
import numpy as np
from jax import lax
from jax.experimental import pallas as pl
from jax.experimental.pallas import tpu as pltpu

F32 = jnp.float32
BF16 = jnp.bfloat16

D_MODEL = 2048
GRID_W = 64
ROPE_THETA = 10000.0
N_Q_HEADS = 8
N_KV_HEADS = 2
HEAD_DIM = 128
GQA_GROUP = N_Q_HEADS // N_KV_HEADS
ATTN_WIDTH = N_Q_HEADS * HEAD_DIM
KV_WIDTH = N_KV_HEADS * HEAD_DIM
SGU_GROUPS = 8
SGU_CHUNK = 128
SGU_WIDTH = SGU_GROUPS * 128
PEER_HEADS = 8
PEER_N_KEYS = 128
PEER_N_EXPERTS = PEER_N_KEYS * PEER_N_KEYS
PEER_QUERY_DIM = 256
PEER_HALF = PEER_QUERY_DIM // 2
PEER_TOPK = 16
EPS = 1e-6

LANES = 128
SUBLANES = 8
VMEM_LIMIT = 56 * 1024 * 1024

N_RANK = PEER_TOPK + 1
RANK_ROWS = 24
N_CHAINS = 4


def _gelu(x):
    c = math.sqrt(2.0 / math.pi)
    return 0.5 * x * (1.0 + jnp.tanh(c * (x + 0.044715 * (x * x * x))))


def _sigmoid(x):
    return 1.0 / (1.0 + jnp.exp(-x))


def _params(sem, vmem=VMEM_LIMIT):
    return pltpu.CompilerParams(dimension_semantics=sem, vmem_limit_bytes=vmem)


def _rmsnorm_kernel(x_ref, g_ref, o_ref):
    x = x_ref[...]
    ms = jnp.mean(x * x, axis=-1, keepdims=True)
    o_ref[...] = (x * lax.rsqrt(ms + EPS) * g_ref[...]).astype(o_ref.dtype)


def _rmsnorm(x, g, tm=512):
    s, d = x.shape
    return pl.pallas_call(
        _rmsnorm_kernel,
        grid=(s // tm,),
        in_specs=[pl.BlockSpec((tm, d), lambda i: (i, 0)),
                  pl.BlockSpec((1, d), lambda i: (0, 0))],
        out_specs=pl.BlockSpec((tm, d), lambda i: (i, 0)),
        out_shape=jax.ShapeDtypeStruct((s, d), BF16),
        compiler_params=_params(("parallel",)),
        name="rmsnorm",
    )(x, g)


def _qkv_kernel(hn_ref, w_ref, qg_ref, kg_ref, c_ref, s_ref, q_ref, k_ref, v_ref):
    z = jnp.dot(hn_ref[...], w_ref[...], preferred_element_type=F32)
    c = c_ref[...]
    s = s_ref[...]

    def norm_rope(xh, g):
        ms = jnp.mean(xh * xh, axis=-1, keepdims=True)
        y = xh * lax.rsqrt(ms + EPS) * g
        return y * c + pltpu.roll(y, HEAD_DIM // 2, axis=1) * s

    scale = 1.0 / math.sqrt(HEAD_DIM)
    for h in range(N_Q_HEADS):
        sl = slice(h * HEAD_DIM, (h + 1) * HEAD_DIM)
        q_ref[:, sl] = (norm_rope(z[:, sl], qg_ref[...]) * scale).astype(q_ref.dtype)
    for h in range(N_KV_HEADS):
        sl = slice(h * HEAD_DIM, (h + 1) * HEAD_DIM)
        zs = slice(ATTN_WIDTH + h * HEAD_DIM, ATTN_WIDTH + (h + 1) * HEAD_DIM)
        k_ref[:, sl] = norm_rope(z[:, zs], kg_ref[...]).astype(k_ref.dtype)
    v_ref[...] = z[:, ATTN_WIDTH + KV_WIDTH:].astype(v_ref.dtype)


def _qkv(hn, w_qkv, qg, kg, cos2, sin2, tm=512):
    s, d = hn.shape
    n = w_qkv.shape[1]
    return pl.pallas_call(
        _qkv_kernel,
        grid=(s // tm,),
        in_specs=[pl.BlockSpec((tm, d), lambda i: (i, 0)),
                  pl.BlockSpec((d, n), lambda i: (0, 0)),
                  pl.BlockSpec((1, HEAD_DIM), lambda i: (0, 0)),
                  pl.BlockSpec((1, HEAD_DIM), lambda i: (0, 0)),
                  pl.BlockSpec((tm, HEAD_DIM), lambda i: (i, 0)),
                  pl.BlockSpec((tm, HEAD_DIM), lambda i: (i, 0))],
        out_specs=[pl.BlockSpec((tm, ATTN_WIDTH), lambda i: (i, 0)),
                   pl.BlockSpec((tm, KV_WIDTH), lambda i: (i, 0)),
                   pl.BlockSpec((tm, KV_WIDTH), lambda i: (i, 0))],
        out_shape=[jax.ShapeDtypeStruct((s, ATTN_WIDTH), BF16),
                   jax.ShapeDtypeStruct((s, KV_WIDTH), BF16),
                   jax.ShapeDtypeStruct((s, KV_WIDTH), BF16)],
        compiler_params=_params(("parallel",)),
        name="qkv_rope",
    )(hn, w_qkv, qg, kg, cos2, sin2)


def _attn_kernel(q_ref, k_ref, v_ref, o_ref, m_ref, l_ref, acc_ref):
    ki = pl.program_id(2)
    tk = k_ref.shape[0]

    @pl.when(ki == 0)
    def _():
        m_ref[...] = jnp.full(m_ref.shape, -jnp.inf, F32)
        l_ref[...] = jnp.zeros(l_ref.shape, F32)
        acc_ref[...] = jnp.zeros(acc_ref.shape, F32)

    k = k_ref[...]
    v = v_ref[...]
    for g in range(GQA_GROUP):
        q = q_ref[:, g * HEAD_DIM:(g + 1) * HEAD_DIM]
        s = lax.dot_general(q, k, (((1,), (1,)), ((), ())), preferred_element_type=F32)
        m_prev = m_ref[g]
        m_new = jnp.maximum(m_prev, jnp.max(s, axis=1, keepdims=True))
        alpha = jnp.exp(m_prev - m_new)
        p = jnp.exp(s - jnp.tile(m_new, (1, tk // LANES)))
        l_ref[g] = alpha * l_ref[g] + jnp.sum(p, axis=1, keepdims=True)
        acc_ref[g] = alpha * acc_ref[g] + jnp.dot(p.astype(v.dtype), v, preferred_element_type=F32)
        m_ref[g] = m_new

    @pl.when(ki == pl.num_programs(2) - 1)
    def _():
        for g in range(GQA_GROUP):
            o_ref[:, g * HEAD_DIM:(g + 1) * HEAD_DIM] = (acc_ref[g] / l_ref[g]).astype(o_ref.dtype)


def _attention(q, k, v, tq=512, tk=2048):
    s = q.shape[0]
    gw = GQA_GROUP * HEAD_DIM
    return pl.pallas_call(
        _attn_kernel,
        grid=(N_KV_HEADS, s // tq, s // tk),
        in_specs=[pl.BlockSpec((tq, gw), lambda h, i, j: (i, h)),
                  pl.BlockSpec((tk, HEAD_DIM), lambda h, i, j: (j, h)),
                  pl.BlockSpec((tk, HEAD_DIM), lambda h, i, j: (j, h))],
        out_specs=pl.BlockSpec((tq, gw), lambda h, i, j: (i, h)),
        out_shape=jax.ShapeDtypeStruct((s, ATTN_WIDTH), BF16),
        scratch_shapes=[pltpu.VMEM((GQA_GROUP, tq, LANES), F32),
                        pltpu.VMEM((GQA_GROUP, tq, LANES), F32),
                        pltpu.VMEM((GQA_GROUP, tq, HEAD_DIM), F32)],
        compiler_params=_params(("parallel", "parallel", "arbitrary")),
        name="flash_attention",
    )(q, k, v)


def _sgu_kernel(hn_ref, wu_ref, wv_ref, gn_ref, ws_ref, bs_ref, o_ref):
    hn = hn_ref[...]
    tm = hn.shape[0]
    u = _gelu(jnp.dot(hn, wu_ref[...], preferred_element_type=F32))
    v = _gelu(jnp.dot(hn, wv_ref[...], preferred_element_type=F32))
    ms = jnp.mean(v * v, axis=-1, keepdims=True)
    vn = (v * lax.rsqrt(ms + EPS) * gn_ref[...]).astype(BF16)
    for c in range(tm // SGU_CHUNK):
        rows = slice(c * SGU_CHUNK, (c + 1) * SGU_CHUNK)
        for g in range(SGU_GROUPS):
            cols = slice(g * 128, (g + 1) * 128)
            mixed = jnp.dot(ws_ref[g], vn[rows, cols], preferred_element_type=F32) + bs_ref[g]
            o_ref[rows, cols] = (u[rows, cols] * mixed).astype(o_ref.dtype)


def _sgu(hn, wu, wv, gn, ws, bs, tm=256):
    s, d = hn.shape
    return pl.pallas_call(
        _sgu_kernel,
        grid=(s // tm,),
        in_specs=[pl.BlockSpec((tm, d), lambda i: (i, 0)),
                  pl.BlockSpec((d, SGU_WIDTH), lambda i: (0, 0)),
                  pl.BlockSpec((d, SGU_WIDTH), lambda i: (0, 0)),
                  pl.BlockSpec((1, SGU_WIDTH), lambda i: (0, 0)),
                  pl.BlockSpec((SGU_GROUPS, SGU_CHUNK, SGU_CHUNK), lambda i: (0, 0, 0)),
                  pl.BlockSpec((SGU_GROUPS, SGU_CHUNK, 128), lambda i: (0, 0, 0))],
        out_specs=pl.BlockSpec((tm, SGU_WIDTH), lambda i: (i, 0)),
        out_shape=jax.ShapeDtypeStruct((s, SGU_WIDTH), BF16),
        compiler_params=_params(("parallel",)),
        name="sgu",
    )(hn, wu, wv, gn, ws, bs)


def _merge_kernel(hn_ref, a_ref, s_ref, wga_ref, wgs_ref, ba_ref, bs_ref, pa_ref, pg_ref, o_ref):
    hn = hn_ref[...]
    ga = _sigmoid(jnp.dot(hn, wga_ref[...], preferred_element_type=F32) + ba_ref[...])
    gs = _sigmoid(jnp.dot(hn, wgs_ref[...], preferred_element_type=F32) + bs_ref[...])
    ya = jnp.dot(a_ref[...], pa_ref[...], preferred_element_type=F32)
    ys = jnp.dot(s_ref[...], pg_ref[...], preferred_element_type=F32)
    o_ref[...] = (ga * ya + gs * ys).astype(o_ref.dtype)


def _merge(hn, attn, sgu, wga, wgs, ba, bs, pa, pg, tm=512, tn=512):
    s, d = hn.shape
    return pl.pallas_call(
        _merge_kernel,
        grid=(d // tn, s // tm),
        in_specs=[pl.BlockSpec((tm, d), lambda j, i: (i, 0)),
                  pl.BlockSpec((tm, ATTN_WIDTH), lambda j, i: (i, 0)),
                  pl.BlockSpec((tm, SGU_WIDTH), lambda j, i: (i, 0)),
                  pl.BlockSpec((d, tn), lambda j, i: (0, j)),
                  pl.BlockSpec((d, tn), lambda j, i: (0, j)),
                  pl.BlockSpec((1, tn), lambda j, i: (0, j)),
                  pl.BlockSpec((1, tn), lambda j, i: (0, j)),
                  pl.BlockSpec((ATTN_WIDTH, tn), lambda j, i: (0, j)),
                  pl.BlockSpec((SGU_WIDTH, tn), lambda j, i: (0, j))],
        out_specs=pl.BlockSpec((tm, tn), lambda j, i: (i, j)),
        out_shape=jax.ShapeDtypeStruct((s, d), BF16),
        compiler_params=_params(("parallel", "parallel")),
        name="gated_merge",
    )(hn, attn, sgu, wga, wgs, ba, bs, pa, pg)


def _out_kernel(x_ref, mg_ref, wo_ref, g_ref, h_ref, xn_ref):
    h = x_ref[...] + jnp.dot(mg_ref[...], wo_ref[...], preferred_element_type=F32)
    h_ref[...] = h
    ms = jnp.mean(h * h, axis=-1, keepdims=True)
    xn_ref[...] = (h * lax.rsqrt(ms + EPS) * g_ref[...]).astype(xn_ref.dtype)


def _out_proj(x, merged, wo, g, tm=256):
    s, d = x.shape
    return pl.pallas_call(
        _out_kernel,
        grid=(s // tm,),
        in_specs=[pl.BlockSpec((tm, d), lambda i: (i, 0)),
                  pl.BlockSpec((tm, d), lambda i: (i, 0)),
                  pl.BlockSpec((d, d), lambda i: (0, 0)),
                  pl.BlockSpec((1, d), lambda i: (0, 0))],
        out_specs=[pl.BlockSpec((tm, d), lambda i: (i, 0)),
                   pl.BlockSpec((tm, d), lambda i: (i, 0))],
        out_shape=[jax.ShapeDtypeStruct((s, d), F32),
                   jax.ShapeDtypeStruct((s, d), BF16)],
        compiler_params=_params(("parallel",)),
        name="out_proj_norm",
    )(x, merged, wo, g)


def _rank_rows(x, k, out_ref):
    rows = x.shape[0]
    iota = lax.broadcasted_iota(jnp.int32, x.shape, 0)
    for r in range(k):
        m = jnp.max(x, axis=0, keepdims=True)
        first = jnp.min(jnp.where(x == m, iota, rows), axis=0, keepdims=True)
        x = jnp.where(iota == first, -jnp.inf, x)
        out_ref[r:r + 1, :] = m


def _peer_route_kernel(xn_ref, wq_ref, k1_ref, k2_ref,
                       a1_ref, p1_ref, s2_ref, p2_ref,
                       st_scr, v1_scr, v2_scr, cand_scr, top_scr):
    pq = jnp.dot(xn_ref[...], wq_ref[...], preferred_element_type=F32).astype(BF16)
    nt = (((1,), (1,)), ((), ()))
    for h in range(PEER_HEADS):
        base = h * PEER_QUERY_DIM
        st_scr[2 * h] = lax.dot_general(k1_ref[h], pq[:, base:base + PEER_HALF], nt,
                                        preferred_element_type=F32)
        st_scr[2 * h + 1] = lax.dot_general(k2_ref[h], pq[:, base + PEER_HALF:base + PEER_QUERY_DIM], nt,
                                            preferred_element_type=F32)

    neg = jnp.full(v1_scr.shape, -jnp.inf, F32)
    v1_scr[...] = neg
    v2_scr[...] = neg

    def per_head(h, carry):
        s1 = st_scr[2 * h]
        s2 = st_scr[2 * h + 1]
        _rank_rows(s1, N_RANK, v1_scr)
        _rank_rows(s2, N_RANK, v2_scr)
        cand_scr[0:RANK_ROWS, :] = v1_scr[0:1, :] + v2_scr[...]
        for i in range(1, SUBLANES):
            lo = RANK_ROWS + (i - 1) * SUBLANES
            cand_scr[lo:lo + SUBLANES, :] = v1_scr[i:i + 1, :] + v2_scr[0:SUBLANES, :]
        lo = RANK_ROWS + (SUBLANES - 1) * SUBLANES
        cand_scr[lo:lo + 2 * SUBLANES, :] = v1_scr[SUBLANES:RANK_ROWS, :] + v2_scr[0:1, :]
        _rank_rows(cand_scr[...], N_RANK, top_scr)
        top = top_scr[...]
        best = top[0:1, :]
        cut = 0.5 * (top[PEER_TOPK - 1:PEER_TOPK, :] + top[PEER_TOPK:PEER_TOPK + 1, :])
        z = jnp.sum(jnp.exp(top[0:PEER_TOPK, :] - best), axis=0, keepdims=True)
        a1_ref[h] = cut - s1
        s2_ref[h] = s2
        p1_ref[h] = jnp.exp(s1 - v1_scr[0:1, :])
        p2_ref[h] = jnp.exp(s2 - v2_scr[0:1, :]) / z
        return carry

    lax.fori_loop(0, PEER_HEADS, per_head, 0)


def _peer_route(xn, wq, k1, k2, tm=256):
    s, d = xn.shape
    stat = jax.ShapeDtypeStruct((PEER_HEADS, PEER_N_KEYS, s), F32)
    stat_spec = pl.BlockSpec((PEER_HEADS, PEER_N_KEYS, tm), lambda i: (0, 0, i))
    cand_rows = RANK_ROWS + (SUBLANES - 1) * SUBLANES + 2 * SUBLANES
    return pl.pallas_call(
        _peer_route_kernel,
        grid=(s // tm,),
        in_specs=[pl.BlockSpec((tm, d), lambda i: (i, 0)),
                  pl.BlockSpec((d, PEER_HEADS * PEER_QUERY_DIM), lambda i: (0, 0)),
                  pl.BlockSpec((PEER_HEADS, PEER_N_KEYS, PEER_HALF), lambda i: (0, 0, 0)),
                  pl.BlockSpec((PEER_HEADS, PEER_N_KEYS, PEER_HALF), lambda i: (0, 0, 0))],
        out_specs=[stat_spec, stat_spec, stat_spec, stat_spec],
        out_shape=[stat, stat, stat, stat],
        scratch_shapes=[pltpu.VMEM((2 * PEER_HEADS, PEER_N_KEYS, tm), F32),
                        pltpu.VMEM((RANK_ROWS, tm), F32),
                        pltpu.VMEM((RANK_ROWS, tm), F32),
                        pltpu.VMEM((cand_rows, tm), F32),
                        pltpu.VMEM((RANK_ROWS, tm), F32)],
        compiler_params=_params(("parallel",)),
        name="peer_route",
    )(xn, wq, k1, k2)


def _peer_dense_kernel(xn_ref, h_ref, a1_ref, p1_ref, s2_ref, p2_ref, u_ref, v_ref, o_ref, at_scr):
    eb = pl.program_id(1)
    tt = xn_ref.shape[0]
    n_e1 = a1_ref.shape[1]
    e1_per_chain = n_e1 // N_CHAINS
    ce = e1_per_chain * PEER_N_KEYS

    @pl.when(eb == 0)
    def _():
        o_ref[...] = h_ref[...]

    xn = xn_ref[...]

    def pre_activations(c):
        return lax.dot_general(xn, u_ref[c * ce:(c + 1) * ce, :], (((1,), (1,)), ((), ())),
                               preferred_element_type=F32)

    st_next = pre_activations(0)
    for c in range(N_CHAINS):
        es = slice(c * ce, (c + 1) * ce)
        st = st_next
        if c + 1 < N_CHAINS:
            st_next = pre_activations(c + 1)
        for jj in range(e1_per_chain):
            j = c * e1_per_chain + jj
            for lt in range(tt // LANES):
                cols = slice(lt * LANES, (lt + 1) * LANES)
                sb = st[lt * LANES:(lt + 1) * LANES, jj * PEER_N_KEYS:(jj + 1) * PEER_N_KEYS]
                w = jnp.zeros((PEER_N_KEYS, LANES), F32)
                for h in range(PEER_HEADS):
                    keep = s2_ref[h, :, cols] >= a1_ref[h, j:j + 1, cols]
                    w = w + p1_ref[h, j:j + 1, cols] * jnp.where(keep, p2_ref[h, :, cols], 0.0)
                at_scr[cols, j * PEER_N_KEYS:(j + 1) * PEER_N_KEYS] = (_gelu(sb) * w.T).astype(at_scr.dtype)
        o_ref[...] += jnp.dot(at_scr[:, es], v_ref[es, :], preferred_element_type=F32)


def _peer_dense(xn, h1, a1, p1, s2, p2, u, v, tt=512, te=1024):
    s, d = xn.shape
    n_e1 = te // PEER_N_KEYS
    once = pl.Buffered(1)
    return pl.pallas_call(
        _peer_dense_kernel,
        grid=(s // tt, PEER_N_EXPERTS // te),
        in_specs=[pl.BlockSpec((tt, d), lambda t, e: (t, 0), pipeline_mode=once),
                  pl.BlockSpec((tt, d), lambda t, e: (t, 0), pipeline_mode=once),
                  pl.BlockSpec((PEER_HEADS, n_e1, tt), lambda t, e: (0, e, t)),
                  pl.BlockSpec((PEER_HEADS, n_e1, tt), lambda t, e: (0, e, t)),
                  pl.BlockSpec((PEER_HEADS, PEER_N_KEYS, tt), lambda t, e: (0, 0, t), pipeline_mode=once),
                  pl.BlockSpec((PEER_HEADS, PEER_N_KEYS, tt), lambda t, e: (0, 0, t), pipeline_mode=once),
                  pl.BlockSpec((te, d), lambda t, e: (e, 0)),
                  pl.BlockSpec((te, d), lambda t, e: (e, 0))],
        out_specs=pl.BlockSpec((tt, d), lambda t, e: (t, 0)),
        out_shape=jax.ShapeDtypeStruct((s, d), F32),
        scratch_shapes=[pltpu.VMEM((tt, te), BF16)],
        compiler_params=_params(("parallel", "arbitrary")),
        name="peer_dense",
    )(xn, h1, a1, p1, s2, p2, u, v)


def _rope_tables(seq_len):
    rows = seq_len // GRID_W
    row = jnp.repeat(jnp.arange(rows, dtype=F32), GRID_W)
    col = jnp.tile(jnp.arange(GRID_W, dtype=F32), rows)
    n_pairs = HEAD_DIM // 4
    inv_freq = ROPE_THETA ** (-jnp.arange(n_pairs, dtype=F32) / n_pairs)
    ang = jnp.concatenate([row[:, None] * inv_freq, col[:, None] * inv_freq], axis=-1)
    cos, sin = jnp.cos(ang), jnp.sin(ang)
    return jnp.concatenate([cos, cos], axis=-1), jnp.concatenate([-sin, sin], axis=-1)


def kernel(x, norm_mix, w_in, b_gate, q_norm, k_norm, sgu_norm, w_sgu, b_sgu, w_attn_proj,
           w_sgu_proj, w_out, norm_ffn, w_peer_q, peer_k1, peer_k2, peer_u, peer_v):
    b, s, d = x.shape
    assert b == 1 and norm_mix.shape[0] == 1
    l = 0
    xs = x.reshape(s, d)

    pair_perm = np.concatenate([np.arange(0, HEAD_DIM, 2), np.arange(1, HEAD_DIM, 2)])
    q_cols = (np.arange(N_Q_HEADS)[:, None] * HEAD_DIM + pair_perm[None, :]).reshape(-1)
    k_cols = ATTN_WIDTH + (np.arange(N_KV_HEADS)[:, None] * HEAD_DIM + pair_perm[None, :]).reshape(-1)
    v_cols = ATTN_WIDTH + KV_WIDTH + np.arange(KV_WIDTH)
    w = w_in[l]
    o_su = ATTN_WIDTH + 2 * KV_WIDTH
    o_sv = o_su + SGU_WIDTH
    o_g = o_sv + SGU_WIDTH
    w_qkv = w[:, np.concatenate([q_cols, k_cols, v_cols])].astype(BF16)
    w_su = w[:, o_su:o_sv].astype(BF16)
    w_sv = w[:, o_sv:o_g].astype(BF16)
    w_ga = w[:, o_g:o_g + d].astype(BF16)
    w_gs = w[:, o_g + d:].astype(BF16)
    cos2, sin2 = _rope_tables(s)

    hn = _rmsnorm(xs, norm_mix[l].reshape(1, d))
    q, k, v = _qkv(hn, w_qkv, q_norm[l][pair_perm].reshape(1, HEAD_DIM),
                   k_norm[l][pair_perm].reshape(1, HEAD_DIM), cos2, sin2)
    attn = _attention(q, k, v)
    bias_s = jnp.broadcast_to(b_sgu[l][:, :, None], (SGU_GROUPS, SGU_CHUNK, 128))
    sgu = _sgu(hn, w_su, w_sv, sgu_norm[l].reshape(1, SGU_WIDTH), w_sgu[l].astype(BF16), bias_s)
    merged = _merge(hn, attn, sgu, w_ga, w_gs, b_gate[l][:d].reshape(1, d), b_gate[l][d:].reshape(1, d),
                    w_attn_proj[l].astype(BF16), w_sgu_proj[l].astype(BF16))
    h1, xn2 = _out_proj(xs, merged, w_out[l].astype(BF16), norm_ffn[l].reshape(1, d))
    a1, p1, s2, p2 = _peer_route(xn2, w_peer_q[l].astype(BF16), peer_k1[l].astype(BF16),
                                 peer_k2[l].astype(BF16))
    out = _peer_dense(xn2, h1, a1, p1, s2, p2, peer_u[l].astype(BF16), peer_v[l].astype(BF16))
    return out.reshape(b, s, d)
```

```python
import functools
import math

import jax
import jax.numpy as jnp
import numpy as np
from jax import lax
from jax.experimental import pallas as pl
from jax.experimental.pallas import tpu as pltpu

F32 = jnp.float32
BF16 = jnp.bfloat16

D_MODEL = 2048
GRID_W = 64
ROPE_THETA = 10000.0
N_Q_HEADS = 8
N_KV_HEADS = 2
HEAD_DIM = 128
GQA_GROUP = N_Q_HEADS // N_KV_HEADS
ATTN_WIDTH = N_Q_HEADS * HEAD_DIM
KV_WIDTH = N_KV_HEADS * HEAD_DIM
SGU_GROUPS = 8
SGU_CHUNK = 128
SGU_WIDTH = SGU_GROUPS * 128
PEER_HEADS = 8
PEER_N_KEYS = 128
PEER_N_EXPERTS = PEER_N_KEYS * PEER_N_KEYS
PEER_QUERY_DIM = 256
PEER_HALF = PEER_QUERY_DIM // 2
PEER_TOPK = 16
EPS = 1e-6

LANES = 128
SUBLANES = 8
VMEM_LIMIT = 56 * 1024 * 1024

N_RANK = PEER_TOPK + 1
ATTN_KEY_CHUNK = 2048
N_CHAINS = 4


def _gelu(x):
    c = math.sqrt(2.0 / math.pi)
    return 0.5 * x * (1.0 + jnp.tanh(c * (x + 0.044715 * (x * x * x))))


def _sigmoid(x):
    return 1.0 / (1.0 + jnp.exp(-x))


def _params(sem, vmem=VMEM_LIMIT):
    return pltpu.CompilerParams(dimension_semantics=sem, vmem_limit_bytes=vmem)


def _rmsnorm_kernel(x_ref, g_ref, o_ref):
    x = x_ref[...]
    ms = jnp.mean(x * x, axis=-1, keepdims=True)
    o_ref[...] = (x * lax.rsqrt(ms + EPS) * g_ref[...]).astype(o_ref.dtype)


def _rmsnorm(x, g, tm=512):
    s, d = x.shape
    return pl.pallas_call(
        _rmsnorm_kernel,
        grid=(s // tm,),
        in_specs=[pl.BlockSpec((tm, d), lambda i: (i, 0)),
                  pl.BlockSpec((1, d), lambda i: (0, 0))],
        out_specs=pl.BlockSpec((tm, d), lambda i: (i, 0)),
        out_shape=jax.ShapeDtypeStruct((s, d), BF16),
        compiler_params=_params(("parallel",)),
        name="rmsnorm",
    )(x, g)


def _qkv_kernel(hn_ref, w_ref, qg_ref, kg_ref, c_ref, s_ref, q_ref, k_ref, v_ref):
    z = jnp.dot(hn_ref[...], w_ref[...], preferred_element_type=F32)
    c = c_ref[...]
    s = s_ref[...]

    def norm_rope(xh, g):
        ms = jnp.mean(xh * xh, axis=-1, keepdims=True)
        y = xh * lax.rsqrt(ms + EPS) * g
        return y * c + pltpu.roll(y, HEAD_DIM // 2, axis=1) * s

    scale = math.log2(math.e) / math.sqrt(HEAD_DIM)
    for h in range(N_Q_HEADS):
        sl = slice(h * HEAD_DIM, (h + 1) * HEAD_DIM)
        q_ref[:, sl] = (norm_rope(z[:, sl], qg_ref[...]) * scale).astype(q_ref.dtype)
    ones = jnp.ones((z.shape[0], HEAD_DIM), v_ref.dtype)
    for h in range(N_KV_HEADS):
        sl = slice(h * HEAD_DIM, (h + 1) * HEAD_DIM)
        zs = slice(ATTN_WIDTH + h * HEAD_DIM, ATTN_WIDTH + (h + 1) * HEAD_DIM)
        k_ref[:, sl] = norm_rope(z[:, zs], kg_ref[...]).astype(k_ref.dtype)
        vs = slice(ATTN_WIDTH + KV_WIDTH + h * HEAD_DIM, ATTN_WIDTH + KV_WIDTH + (h + 1) * HEAD_DIM)
        v_ref[:, 2 * h * HEAD_DIM:(2 * h + 1) * HEAD_DIM] = z[:, vs].astype(v_ref.dtype)
        v_ref[:, (2 * h + 1) * HEAD_DIM:(2 * h + 2) * HEAD_DIM] = ones


def _qkv(hn, w_qkv, qg, kg, cos2, sin2, tm=512):
    s, d = hn.shape
    n = w_qkv.shape[1]
    return pl.pallas_call(
        _qkv_kernel,
        grid=(s // tm,),
        in_specs=[pl.BlockSpec((tm, d), lambda i: (i, 0)),
                  pl.BlockSpec((d, n), lambda i: (0, 0)),
                  pl.BlockSpec((1, HEAD_DIM), lambda i: (0, 0)),
                  pl.BlockSpec((1, HEAD_DIM), lambda i: (0, 0)),
                  pl.BlockSpec((tm, HEAD_DIM), lambda i: (i, 0)),
                  pl.BlockSpec((tm, HEAD_DIM), lambda i: (i, 0))],
        out_specs=[pl.BlockSpec((tm, ATTN_WIDTH), lambda i: (i, 0)),
                   pl.BlockSpec((tm, KV_WIDTH), lambda i: (i, 0)),
                   pl.BlockSpec((tm, 2 * KV_WIDTH), lambda i: (i, 0))],
        out_shape=[jax.ShapeDtypeStruct((s, ATTN_WIDTH), BF16),
                   jax.ShapeDtypeStruct((s, KV_WIDTH), BF16),
                   jax.ShapeDtypeStruct((s, 2 * KV_WIDTH), BF16)],
        compiler_params=_params(("parallel",)),
        name="qkv_rope",
    )(hn, w_qkv, qg, kg, cos2, sin2)


def _attn_kernel(q_ref, k_ref, v_ref, o_ref, m_ref, acc_ref):
    ki = pl.program_id(2)
    tk = k_ref.shape[0]

    @pl.when(ki == 0)
    def _():
        m_ref[...] = jnp.full(m_ref.shape, -jnp.inf, F32)
        acc_ref[...] = jnp.zeros(acc_ref.shape, F32)

    kc = min(ATTN_KEY_CHUNK, tk)
    for c in range(tk // kc):
        k = k_ref[c * kc:(c + 1) * kc, :]
        v = v_ref[c * kc:(c + 1) * kc, :]
        for g in range(GQA_GROUP):
            q = q_ref[:, g * HEAD_DIM:(g + 1) * HEAD_DIM]
            s = lax.dot_general(q, k, (((1,), (1,)), ((), ())), preferred_element_type=F32)
            m_prev = m_ref[g]
            m_new = jnp.maximum(m_prev, jnp.max(s, axis=1, keepdims=True))
            alpha = jnp.exp2(m_prev - m_new)
            p = jnp.exp2(s - jnp.tile(m_new, (1, kc // LANES)))
            acc_ref[g] = jnp.tile(alpha, (1, 2)) * acc_ref[g] + jnp.dot(p.astype(v.dtype), v,
                                                                         preferred_element_type=F32)
            m_ref[g] = m_new

    @pl.when(ki == pl.num_programs(2) - 1)
    def _():
        for g in range(GQA_GROUP):
            acc = acc_ref[g]
            o_ref[:, g * HEAD_DIM:(g + 1) * HEAD_DIM] = (acc[:, :HEAD_DIM] / acc[:, HEAD_DIM:]).astype(o_ref.dtype)


def _attention(q, k, v, tq=512, tk=8192):
    s = q.shape[0]
    gw = GQA_GROUP * HEAD_DIM
    return pl.pallas_call(
        _attn_kernel,
        grid=(N_KV_HEADS, s // tq, s // tk),
        in_specs=[pl.BlockSpec((tq, gw), lambda h, i, j: (i, h)),
                  pl.BlockSpec((tk, HEAD_DIM), lambda h, i, j: (j, h)),
                  pl.BlockSpec((tk, 2 * HEAD_DIM), lambda h, i, j: (j, h))],
        out_specs=pl.BlockSpec((tq, gw), lambda h, i, j: (i, h)),
        out_shape=jax.ShapeDtypeStruct((s, ATTN_WIDTH), BF16),
        scratch_shapes=[pltpu.VMEM((GQA_GROUP, tq, LANES), F32),
                        pltpu.VMEM((GQA_GROUP, tq, 2 * HEAD_DIM), F32)],
        compiler_params=_params(("parallel", "parallel", "arbitrary")),
        name="flash_attention",
    )(q, k, v)


def _sgu_kernel(hn_ref, wu_ref, wv_ref, gn_ref, ws_ref, bs_ref, o_ref):
    hn = hn_ref[...]
    tm = hn.shape[0]
    u = _gelu(jnp.dot(hn, wu_ref[...], preferred_element_type=F32))
    v = _gelu(jnp.dot(hn, wv_ref[...], preferred_element_type=F32))
    ms = jnp.mean(v * v, axis=-1, keepdims=True)
    vn = (v * lax.rsqrt(ms + EPS) * gn_ref[...]).astype(BF16)
    for c in range(tm // SGU_CHUNK):
        rows = slice(c * SGU_CHUNK, (c + 1) * SGU_CHUNK)
        for g in range(SGU_GROUPS):
            cols = slice(g * 128, (g + 1) * 128)
            mixed = jnp.dot(ws_ref[g], vn[rows, cols], preferred_element_type=F32) + bs_ref[g]
            o_ref[rows, cols] = (u[rows, cols] * mixed).astype(o_ref.dtype)


def _sgu(hn, wu, wv, gn, ws, bs, tm=256):
    s, d = hn.shape
    return pl.pallas_call(
        _sgu_kernel,
        grid=(s // tm,),
        in_specs=[pl.BlockSpec((tm, d), lambda i: (i, 0)),
                  pl.BlockSpec((d, SGU_WIDTH), lambda i: (0, 0)),
                  pl.BlockSpec((d, SGU_WIDTH), lambda i: (0, 0)),
                  pl.BlockSpec((1, SGU_WIDTH), lambda i: (0, 0)),
                  pl.BlockSpec((SGU_GROUPS, SGU_CHUNK, SGU_CHUNK), lambda i: (0, 0, 0)),
                  pl.BlockSpec((SGU_GROUPS, SGU_CHUNK, 128), lambda i: (0, 0, 0))],
        out_specs=pl.BlockSpec((tm, SGU_WIDTH), lambda i: (i, 0)),
        out_shape=jax.ShapeDtypeStruct((s, SGU_WIDTH), BF16),
        compiler_params=_params(("parallel",)),
        name="sgu",
    )(hn, wu, wv, gn, ws, bs)


def _merge_kernel(hn_ref, a_ref, s_ref, wga_ref, wgs_ref, ba_ref, bs_ref, pa_ref, pg_ref, o_ref):
    hn = hn_ref[...]
    ga = _sigmoid(jnp.dot(hn, wga_ref[...], preferred_element_type=F32) + ba_ref[...])
    gs = _sigmoid(jnp.dot(hn, wgs_ref[...], preferred_element_type=F32) + bs_ref[...])
    ya = jnp.dot(a_ref[...], pa_ref[...], preferred_element_type=F32)
    ys = jnp.dot(s_ref[...], pg_ref[...], preferred_element_type=F32)
    o_ref[...] = (ga * ya + gs * ys).astype(o_ref.dtype)


def _merge(hn, attn, sgu, wga, wgs, ba, bs, pa, pg, tm=512, tn=512):
    s, d = hn.shape
    return pl.pallas_call(
        _merge_kernel,
        grid=(d // tn, s // tm),
        in_specs=[pl.BlockSpec((tm, d), lambda j, i: (i, 0)),
                  pl.BlockSpec((tm, ATTN_WIDTH), lambda j, i: (i, 0)),
                  pl.BlockSpec((tm, SGU_WIDTH), lambda j, i: (i, 0)),
                  pl.BlockSpec((d, tn), lambda j, i: (0, j)),
                  pl.BlockSpec((d, tn), lambda j, i: (0, j)),
                  pl.BlockSpec((1, tn), lambda j, i: (0, j)),
                  pl.BlockSpec((1, tn), lambda j, i: (0, j)),
                  pl.BlockSpec((ATTN_WIDTH, tn), lambda j, i: (0, j)),
                  pl.BlockSpec((SGU_WIDTH, tn), lambda j, i: (0, j))],
        out_specs=pl.BlockSpec((tm, tn), lambda j, i: (i, j)),
        out_shape=jax.ShapeDtypeStruct((s, d), BF16),
        compiler_params=_params(("parallel", "parallel")),
        name="gated_merge",
    )(hn, attn, sgu, wga, wgs, ba, bs, pa, pg)


def _out_kernel(x_ref, mg_ref, wo_ref, g_ref, h_ref, xn_ref):
    h = x_ref[...] + jnp.dot(mg_ref[...], wo_ref[...], preferred_element_type=F32)
    h_ref[...] = h
    ms = jnp.mean(h * h, axis=-1, keepdims=True)
    xn_ref[...] = (h * lax.rsqrt(ms + EPS) * g_ref[...]).astype(xn_ref.dtype)


def _out_proj(x, merged, wo, g, tm=256):
    s, d = x.shape
    return pl.pallas_call(
        _out_kernel,
        grid=(s // tm,),
        in_specs=[pl.BlockSpec((tm, d), lambda i: (i, 0)),
                  pl.BlockSpec((tm, d), lambda i: (i, 0)),
                  pl.BlockSpec((d, d), lambda i: (0, 0)),
                  pl.BlockSpec((1, d), lambda i: (0, 0))],
        out_specs=[pl.BlockSpec((tm, d), lambda i: (i, 0)),
                   pl.BlockSpec((tm, d), lambda i: (i, 0))],
        out_shape=[jax.ShapeDtypeStruct((s, d), F32),
                   jax.ShapeDtypeStruct((s, d), BF16)],
        compiler_params=_params(("parallel",)),
        name="out_proj_norm",
    )(x, merged, wo, g)


def _sorting_network(n):
    def merge(lo, hi, r):
        step = r * 2
        if step < hi - lo:
            yield from merge(lo, hi, step)
            yield from merge(lo + r, hi, step)
            yield from [(i, i + r) for i in range(lo + r, hi - r, step)]
        else:
            yield (lo, lo + r)

    def sort(lo, hi):
        if hi - lo >= 1:
            mid = lo + (hi - lo) // 2
            yield from sort(lo, mid)
            yield from sort(mid + 1, hi)
            yield from merge(lo, hi, 1)

    return tuple(sort(0, n - 1))


def _compare_exchange(x, i, j):
    a, b = x[i], x[j]
    if b is None:
        return
    if a is None:
        x[i], x[j] = b, None
        return
    x[i], x[j] = jnp.maximum(a, b), jnp.minimum(a, b)


def _sort_desc(x):
    for i, j in _sorting_network(len(x)):
        _compare_exchange(x, i, j)
    return x


def _bitonic_sort_desc(x):
    n = len(x)
    stride = n // 2
    while stride >= 1:
        for i in range(n):
            if i & stride == 0:
                _compare_exchange(x, i, i + stride)
        stride //= 2
    return x


def _merge_with_sublane_neighbour(x, shift, keep_all):
    n = len(x)
    y = [None if a is None else pltpu.roll(a, shift, axis=0) for a in x]
    hi, lo = [], []
    for i in range(n):
        a, b = x[i], y[n - 1 - i]
        if a is None or b is None:
            hi.append(b if a is None else a)
            lo.append(None)
        else:
            hi.append(jnp.maximum(a, b))
            lo.append(jnp.minimum(a, b))
    hi = _bitonic_sort_desc(hi)
    return hi + _bitonic_sort_desc(lo) if keep_all else hi


def _rank_over_sublanes_and_tiles(tiles):
    x = _sort_desc(list(tiles))
    x = _merge_with_sublane_neighbour(x, 4, keep_all=True)
    x = _merge_with_sublane_neighbour(x, 2, keep_all=False)
    return _merge_with_sublane_neighbour(x, 1, keep_all=False)


def _peer_route_kernel(xn_ref, wq_ref, k1_ref, k2_ref,
                       a1_ref, p1_ref, s2_ref, p2_ref, st_scr):
    n_lt = st_scr.shape[1]
    pq = jnp.dot(xn_ref[...], wq_ref[...], preferred_element_type=F32).astype(BF16)
    nt = (((1,), (1,)), ((), ()))
    for h in range(PEER_HEADS):
        for half, k_ref in enumerate((k1_ref, k2_ref)):
            lo = h * PEER_QUERY_DIM + half * PEER_HALF
            sc = lax.dot_general(k_ref[h], pq[:, lo:lo + PEER_HALF], nt, preferred_element_type=F32)
            for lt in range(n_lt):
                st_scr[2 * h + half, lt] = sc[:, lt * LANES:(lt + 1) * LANES]

    n_tiles = PEER_N_KEYS // SUBLANES
    sub = lax.broadcasted_iota(jnp.int32, (SUBLANES, LANES), 0)

    def spread(vals):
        out = vals[-1]
        for r in range(len(vals) - 2, -1, -1):
            out = jnp.where(sub == r, vals[r], out)
        return out

    def per_slab(idx, carry):
        h = idx // n_lt
        lt = idx % n_lt
        s1 = st_scr[2 * h, lt]
        s2 = st_scr[2 * h + 1, lt]
        v1 = _rank_over_sublanes_and_tiles([s1[t * SUBLANES:(t + 1) * SUBLANES] for t in range(n_tiles)])
        v2 = _rank_over_sublanes_and_tiles([s2[t * SUBLANES:(t + 1) * SUBLANES] for t in range(n_tiles)])
        v2_lo, v2_hi = spread(v2[0:SUBLANES]), spread(v2[SUBLANES:2 * SUBLANES])
        v1_hi = spread(v1[SUBLANES:2 * SUBLANES])
        corner = jnp.where(sub == 0, v1[0] + v2[PEER_TOPK],
                           jnp.where(sub == 1, v1[PEER_TOPK] + v2[0], -jnp.inf))
        cands = [v1[0] + v2_lo, v1[0] + v2_hi] + [v1[i] + v2_lo for i in range(1, SUBLANES)]
        cands += [v1_hi + v2[0], corner]
        top = _rank_over_sublanes_and_tiles(cands + [None] * (n_tiles - len(cands)))
        cut = 0.5 * (top[PEER_TOPK - 1] + top[PEER_TOPK])
        z = jnp.ones((SUBLANES, LANES), F32)
        for r in range(1, PEER_TOPK):
            z = z + jnp.exp(top[r] - top[0])
        reps = (n_tiles, 1)
        a1_ref[h, lt] = jnp.tile(cut, reps) - s1
        s2_ref[h, lt] = s2
        p1_ref[h, lt] = jnp.exp(s1 - jnp.tile(v1[0], reps))
        p2_ref[h, lt] = jnp.exp(s2 - jnp.tile(v2[0], reps)) * jnp.tile(1.0 / z, reps)
        return carry

    lax.fori_loop(0, PEER_HEADS * n_lt, per_slab, 0)


def _peer_route(xn, wq, k1, k2, tm=512):
    s, d = xn.shape
    n_lt = tm // LANES
    stat = jax.ShapeDtypeStruct((PEER_HEADS, s // LANES, PEER_N_KEYS, LANES), F32)
    stat_spec = pl.BlockSpec((PEER_HEADS, n_lt, PEER_N_KEYS, LANES), lambda i: (0, i, 0, 0))
    return pl.pallas_call(
        _peer_route_kernel,
        grid=(s // tm,),
        in_specs=[pl.BlockSpec((tm, d), lambda i: (i, 0)),
                  pl.BlockSpec((d, PEER_HEADS * PEER_QUERY_DIM), lambda i: (0, 0)),
                  pl.BlockSpec((PEER_HEADS, PEER_N_KEYS, PEER_HALF), lambda i: (0, 0, 0)),
                  pl.BlockSpec((PEER_HEADS, PEER_N_KEYS, PEER_HALF), lambda i: (0, 0, 0))],
        out_specs=[stat_spec, stat_spec, stat_spec, stat_spec],
        out_shape=[stat, stat, stat, stat],
        scratch_shapes=[pltpu.VMEM((2 * PEER_HEADS, n_lt, PEER_N_KEYS, LANES), F32)],
        compiler_params=_params(("parallel",)),
        name="peer_route",
    )(xn, wq, k1, k2)


def _peer_dense_kernel(xn_ref, h_ref, a1_ref, p1_ref, s2_ref, p2_ref, u_ref, v_ref, o_ref, at_scr):
    eb = pl.program_id(1)
    tt = xn_ref.shape[0]
    n_e1 = a1_ref.shape[2]
    e1_per_chain = n_e1 // N_CHAINS
    ce = e1_per_chain * PEER_N_KEYS

    @pl.when(eb == 0)
    def _():
        o_ref[...] = h_ref[...]

    xn = xn_ref[...]

    def pre_activations(c):
        return lax.dot_general(xn, u_ref[c * ce:(c + 1) * ce, :], (((1,), (1,)), ((), ())),
                               preferred_element_type=F32)

    st_next = pre_activations(0)
    for c in range(N_CHAINS):
        es = slice(c * ce, (c + 1) * ce)
        st = st_next
        if c + 1 < N_CHAINS:
            st_next = pre_activations(c + 1)
        for jj in range(e1_per_chain):
            j = c * e1_per_chain + jj
            for lt in range(tt // LANES):
                cols = slice(lt * LANES, (lt + 1) * LANES)
                sb = st[lt * LANES:(lt + 1) * LANES, jj * PEER_N_KEYS:(jj + 1) * PEER_N_KEYS]
                w = jnp.zeros((PEER_N_KEYS, LANES), F32)
                for h in range(PEER_HEADS):
                    keep = s2_ref[h, lt] >= a1_ref[h, lt, j:j + 1, :]
                    w = w + p1_ref[h, lt, j:j + 1, :] * jnp.where(keep, p2_ref[h, lt], 0.0)
                at_scr[cols, j * PEER_N_KEYS:(j + 1) * PEER_N_KEYS] = (_gelu(sb) * w.T).astype(at_scr.dtype)
        o_ref[...] += jnp.dot(at_scr[:, es], v_ref[es, :], preferred_element_type=F32)


def _peer_dense(xn, h1, a1, p1, s2, p2, u, v, tt=512, te=1024):
    s, d = xn.shape
    n_e1 = te // PEER_N_KEYS
    n_lt = tt // LANES
    once = pl.Buffered(1)
    return pl.pallas_call(
        _peer_dense_kernel,
        grid=(s // tt, PEER_N_EXPERTS // te),
        in_specs=[pl.BlockSpec((tt, d), lambda t, e: (t, 0), pipeline_mode=once),
                  pl.BlockSpec((tt, d), lambda t, e: (t, 0), pipeline_mode=once),
                  pl.BlockSpec((PEER_HEADS, n_lt, n_e1, LANES), lambda t, e: (0, t, e, 0)),
                  pl.BlockSpec((PEER_HEADS, n_lt, n_e1, LANES), lambda t, e: (0, t, e, 0)),
                  pl.BlockSpec((PEER_HEADS, n_lt, PEER_N_KEYS, LANES), lambda t, e: (0, t, 0, 0),
                               pipeline_mode=once),
                  pl.BlockSpec((PEER_HEADS, n_lt, PEER_N_KEYS, LANES), lambda t, e: (0, t, 0, 0),
                               pipeline_mode=once),
                  pl.BlockSpec((te, d), lambda t, e: (e, 0)),
                  pl.BlockSpec((te, d), lambda t, e: (e, 0))],
        out_specs=pl.BlockSpec((tt, d), lambda t, e: (t, 0)),
        out_shape=jax.ShapeDtypeStruct((s, d), F32),
        scratch_shapes=[pltpu.VMEM((tt, te), BF16)],
        compiler_params=_params(("parallel", "arbitrary")),
        name="peer_dense",
    )(xn, h1, a1, p1, s2, p2, u, v)


def _rope_tables(seq_len):
    rows = seq_len // GRID_W
    row = jnp.repeat(jnp.arange(rows, dtype=F32), GRID_W)
    col = jnp.tile(jnp.arange(GRID_W, dtype=F32), rows)
    n_pairs = HEAD_DIM // 4
    inv_freq = ROPE_THETA ** (-jnp.arange(n_pairs, dtype=F32) / n_pairs)
    ang = jnp.concatenate([row[:, None] * inv_freq, col[:, None] * inv_freq], axis=-1)
    cos, sin = jnp.cos(ang), jnp.sin(ang)
    return jnp.concatenate([cos, cos], axis=-1), jnp.concatenate([-sin, sin], axis=-1)


def kernel(x, norm_mix, w_in, b_gate, q_norm, k_norm, sgu_norm, w_sgu, b_sgu, w_attn_proj,
           w_sgu_proj, w_out, norm_ffn, w_peer_q, peer_k1, peer_k2, peer_u, peer_v):
    b, s, d = x.shape
    assert b == 1 and norm_mix.shape[0] == 1
    l = 0
    xs = x.reshape(s, d)

    pair_perm = np.concatenate([np.arange(0, HEAD_DIM, 2), np.arange(1, HEAD_DIM, 2)])
    q_cols = (np.arange(N_Q_HEADS)[:, None] * HEAD_DIM + pair_perm[None, :]).reshape(-1)
    k_cols = ATTN_WIDTH + (np.arange(N_KV_HEADS)[:, None] * HEAD_DIM + pair_perm[None, :]).reshape(-1)
    v_cols = ATTN_WIDTH + KV_WIDTH + np.arange(KV_WIDTH)
    w = w_in[l]
    o_su = ATTN_WIDTH + 2 * KV_WIDTH
    o_sv = o_su + SGU_WIDTH
    o_g = o_sv + SGU_WIDTH
    w_qkv = w[:, np.concatenate([q_cols, k_cols, v_cols])].astype(BF16)
    w_su = w[:, o_su:o_sv].astype(BF16)
    w_sv = w[:, o_sv:o_g].astype(BF16)
    w_ga = w[:, o_g:o_g + d].astype(BF16)
    w_gs = w[:, o_g + d:].astype(BF16)
    cos2, sin2 = _rope_tables(s)

    hn = _rmsnorm(xs, norm_mix[l].reshape(1, d))
    q, k, v = _qkv(hn, w_qkv, q_norm[l][pair_perm].reshape(1, HEAD_DIM),
                   k_norm[l][pair_perm].reshape(1, HEAD_DIM), cos2, sin2)
    attn = _attention(q, k, v)
    bias_s = jnp.broadcast_to(b_sgu[l][:, :, None], (SGU_GROUPS, SGU_CHUNK, 128))
    sgu = _sgu(hn, w_su, w_sv, sgu_norm[l].reshape(1, SGU_WIDTH), w_sgu[l].astype(BF16), bias_s)
    merged = _merge(hn, attn, sgu, w_ga, w_gs, b_gate[l][:d].reshape(1, d), b_gate[l][d:].reshape(1, d),
                    w_attn_proj[l].astype(BF16), w_sgu_proj[l].astype(BF16))
    h1, xn2 = _out_proj(xs, merged, w_out[l].astype(BF16), norm_ffn[l].reshape(1, d))
    a1, p1, s2, p2 = _peer_route(xn2, w_peer_q[l].astype(BF16), peer_k1[l].astype(BF16),
                                 peer_k2[l].astype(BF16))
    out = _peer_dense(xn2, h1, a1, p1, s2, p2, peer_u[l].astype(BF16), peer_v[l].astype(BF16))
    return out.reshape(b, s, d)
```

```python
import functools
import math

import jax
import jax.numpy as jnp
import numpy as np
from jax import lax
from jax.experimental import pallas as pl
from jax.experimental.pallas import tpu as pltpu

F32 = jnp.float32
BF16 = jnp.bfloat16

D_MODEL = 2048
GRID_W = 64
ROPE_THETA = 10000.0
N_Q_HEADS = 8
N_KV_HEADS = 2
HEAD_DIM = 128
GQA_GROUP = N_Q_HEADS // N_KV_HEADS
ATTN_WIDTH = N_Q_HEADS * HEAD_DIM
KV_WIDTH = N_KV_HEADS * HEAD_DIM
SGU_GROUPS = 8
SGU_CHUNK = 128
SGU_WIDTH = SGU_GROUPS * 128
PEER_HEADS = 8
PEER_N_KEYS = 128
PEER_N_EXPERTS = PEER_N_KEYS * PEER_N_KEYS
PEER_QUERY_DIM = 256
PEER_HALF = PEER_QUERY_DIM // 2
PEER_TOPK = 16
EPS = 1e-6

LANES = 128
SUBLANES = 8
VMEM_LIMIT = 56 * 1024 * 1024

N_RANK = PEER_TOPK + 1
ATTN_KEY_CHUNK = 2048
N_CHAINS = 4


def _gelu(x):
    c = math.sqrt(2.0 / math.pi)
    return 0.5 * x * (1.0 + jnp.tanh(c * (x + 0.044715 * (x * x * x))))


def _sigmoid(x):
    return 1.0 / (1.0 + jnp.exp(-x))


def _params(sem, vmem=VMEM_LIMIT):
    return pltpu.CompilerParams(dimension_semantics=sem, vmem_limit_bytes=vmem)


def _qkv_kernel(x_ref, g_ref, w_ref, qg_ref, kg_ref, c_ref, s_ref, hn_ref, q_ref, k_ref, v_ref):
    x = x_ref[...]
    ms = jnp.mean(x * x, axis=-1, keepdims=True)
    hn = (x * lax.rsqrt(ms + EPS) * g_ref[...]).astype(hn_ref.dtype)
    hn_ref[...] = hn
    z = jnp.dot(hn, w_ref[...], preferred_element_type=F32)
    c = c_ref[...]
    s = s_ref[...]

    def norm_rope(xh, g):
        ms = jnp.mean(xh * xh, axis=-1, keepdims=True)
        y = xh * lax.rsqrt(ms + EPS) * g
        return y * c + pltpu.roll(y, HEAD_DIM // 2, axis=1) * s

    scale = math.log2(math.e) / math.sqrt(HEAD_DIM)
    for h in range(N_Q_HEADS):
        sl = slice(h * HEAD_DIM, (h + 1) * HEAD_DIM)
        q_ref[:, sl] = (norm_rope(z[:, sl], qg_ref[...]) * scale).astype(q_ref.dtype)
    ones = jnp.ones((z.shape[0], HEAD_DIM), v_ref.dtype)
    for h in range(N_KV_HEADS):
        sl = slice(h * HEAD_DIM, (h + 1) * HEAD_DIM)
        zs = slice(ATTN_WIDTH + h * HEAD_DIM, ATTN_WIDTH + (h + 1) * HEAD_DIM)
        k_ref[:, sl] = norm_rope(z[:, zs], kg_ref[...]).astype(k_ref.dtype)
        vs = slice(ATTN_WIDTH + KV_WIDTH + h * HEAD_DIM, ATTN_WIDTH + KV_WIDTH + (h + 1) * HEAD_DIM)
        v_ref[:, 2 * h * HEAD_DIM:(2 * h + 1) * HEAD_DIM] = z[:, vs].astype(v_ref.dtype)
        v_ref[:, (2 * h + 1) * HEAD_DIM:(2 * h + 2) * HEAD_DIM] = ones


def _norm_qkv(x, g, w_qkv, qg, kg, cos2, sin2, tm=512):
    s, d = x.shape
    n = w_qkv.shape[1]
    return pl.pallas_call(
        _qkv_kernel,
        grid=(s // tm,),
        in_specs=[pl.BlockSpec((tm, d), lambda i: (i, 0)),
                  pl.BlockSpec((1, d), lambda i: (0, 0)),
                  pl.BlockSpec((d, n), lambda i: (0, 0)),
                  pl.BlockSpec((1, HEAD_DIM), lambda i: (0, 0)),
                  pl.BlockSpec((1, HEAD_DIM), lambda i: (0, 0)),
                  pl.BlockSpec((tm, HEAD_DIM), lambda i: (i, 0)),
                  pl.BlockSpec((tm, HEAD_DIM), lambda i: (i, 0))],
        out_specs=[pl.BlockSpec((tm, d), lambda i: (i, 0)),
                   pl.BlockSpec((tm, ATTN_WIDTH), lambda i: (i, 0)),
                   pl.BlockSpec((tm, KV_WIDTH), lambda i: (i, 0)),
                   pl.BlockSpec((tm, 2 * KV_WIDTH), lambda i: (i, 0))],
        out_shape=[jax.ShapeDtypeStruct((s, d), BF16),
                   jax.ShapeDtypeStruct((s, ATTN_WIDTH), BF16),
                   jax.ShapeDtypeStruct((s, KV_WIDTH), BF16),
                   jax.ShapeDtypeStruct((s, 2 * KV_WIDTH), BF16)],
        compiler_params=_params(("parallel",)),
        name="norm_qkv_rope",
    )(x, g, w_qkv, qg, kg, cos2, sin2)


def _key_chunks(tk):
    edge = ATTN_KEY_CHUNK // 4
    if tk < 2 * ATTN_KEY_CHUNK:
        return [min(tk, ATTN_KEY_CHUNK)] * max(1, tk // ATTN_KEY_CHUNK)
    n_full = tk // ATTN_KEY_CHUNK - 1
    return [edge] + [ATTN_KEY_CHUNK] * n_full + [ATTN_KEY_CHUNK - 2 * edge, edge]


def _attn_kernel(q_ref, k_ref, v_ref, o_ref, m_ref, acc_ref):
    ki = pl.program_id(2)
    tk = k_ref.shape[0]

    @pl.when(ki == 0)
    def _():
        m_ref[...] = jnp.full(m_ref.shape, -jnp.inf, F32)
        acc_ref[...] = jnp.zeros(acc_ref.shape, F32)

    lo = 0
    for kc in _key_chunks(tk):
        k = k_ref[lo:lo + kc, :]
        v = v_ref[lo:lo + kc, :]
        lo += kc
        for g in range(GQA_GROUP):
            q = q_ref[:, g * HEAD_DIM:(g + 1) * HEAD_DIM]
            s = lax.dot_general(q, k, (((1,), (1,)), ((), ())), preferred_element_type=F32)
            m_prev = m_ref[g]
            m_new = jnp.maximum(m_prev, jnp.max(s, axis=1, keepdims=True))
            alpha = jnp.exp2(m_prev - m_new)
            p = jnp.exp2(s - jnp.tile(m_new, (1, kc // LANES)))
            acc_ref[g] = jnp.tile(alpha, (1, 2)) * acc_ref[g] + jnp.dot(p.astype(v.dtype), v,
                                                                         preferred_element_type=F32)
            m_ref[g] = m_new

    @pl.when(ki == pl.num_programs(2) - 1)
    def _():
        for g in range(GQA_GROUP):
            acc = acc_ref[g]
            o_ref[:, g * HEAD_DIM:(g + 1) * HEAD_DIM] = (acc[:, :HEAD_DIM] / acc[:, HEAD_DIM:]).astype(o_ref.dtype)


def _attention(q, k, v, tq=512, tk=8192):
    s = q.shape[0]
    gw = GQA_GROUP * HEAD_DIM
    return pl.pallas_call(
        _attn_kernel,
        grid=(N_KV_HEADS, s // tq, s // tk),
        in_specs=[pl.BlockSpec((tq, gw), lambda h, i, j: (i, h)),
                  pl.BlockSpec((tk, HEAD_DIM), lambda h, i, j: (j, h)),
                  pl.BlockSpec((tk, 2 * HEAD_DIM), lambda h, i, j: (j, h))],
        out_specs=pl.BlockSpec((tq, gw), lambda h, i, j: (i, h)),
        out_shape=jax.ShapeDtypeStruct((s, ATTN_WIDTH), BF16),
        scratch_shapes=[pltpu.VMEM((GQA_GROUP, tq, LANES), F32),
                        pltpu.VMEM((GQA_GROUP, tq, 2 * HEAD_DIM), F32)],
        compiler_params=_params(("parallel", "parallel", "arbitrary")),
        name="flash_attention",
    )(q, k, v)


def _sgu_kernel(hn_ref, wu_ref, wv_ref, gn_ref, ws_ref, bs_ref, o_ref):
    hn = hn_ref[...]
    tm = hn.shape[0]
    u = _gelu(jnp.dot(hn, wu_ref[...], preferred_element_type=F32))
    v = _gelu(jnp.dot(hn, wv_ref[...], preferred_element_type=F32))
    ms = jnp.mean(v * v, axis=-1, keepdims=True)
    vn = (v * lax.rsqrt(ms + EPS) * gn_ref[...]).astype(BF16)
    for c in range(tm // SGU_CHUNK):
        rows = slice(c * SGU_CHUNK, (c + 1) * SGU_CHUNK)
        for g in range(SGU_GROUPS):
            cols = slice(g * 128, (g + 1) * 128)
            mixed = jnp.dot(ws_ref[g], vn[rows, cols], preferred_element_type=F32) + bs_ref[g]
            o_ref[rows, cols] = (u[rows, cols] * mixed).astype(o_ref.dtype)


def _sgu(hn, wu, wv, gn, ws, bs, tm=256):
    s, d = hn.shape
    return pl.pallas_call(
        _sgu_kernel,
        grid=(s // tm,),
        in_specs=[pl.BlockSpec((tm, d), lambda i: (i, 0)),
                  pl.BlockSpec((d, SGU_WIDTH), lambda i: (0, 0)),
                  pl.BlockSpec((d, SGU_WIDTH), lambda i: (0, 0)),
                  pl.BlockSpec((1, SGU_WIDTH), lambda i: (0, 0)),
                  pl.BlockSpec((SGU_GROUPS, SGU_CHUNK, SGU_CHUNK), lambda i: (0, 0, 0)),
                  pl.BlockSpec((SGU_GROUPS, SGU_CHUNK, 128), lambda i: (0, 0, 0))],
        out_specs=pl.BlockSpec((tm, SGU_WIDTH), lambda i: (i, 0)),
        out_shape=jax.ShapeDtypeStruct((s, SGU_WIDTH), BF16),
        compiler_params=_params(("parallel",)),
        name="sgu",
    )(hn, wu, wv, gn, ws, bs)


def _merge_kernel(hn_ref, a_ref, s_ref, wga_ref, wgs_ref, ba_ref, bs_ref, pa_ref, pg_ref, o_ref):
    hn = hn_ref[...]
    ga = _sigmoid(jnp.dot(hn, wga_ref[...], preferred_element_type=F32) + ba_ref[...])
    gs = _sigmoid(jnp.dot(hn, wgs_ref[...], preferred_element_type=F32) + bs_ref[...])
    ya = jnp.dot(a_ref[...], pa_ref[...], preferred_element_type=F32)
    ys = jnp.dot(s_ref[...], pg_ref[...], preferred_element_type=F32)
    o_ref[...] = (ga * ya + gs * ys).astype(o_ref.dtype)


def _merge(hn, attn, sgu, wga, wgs, ba, bs, pa, pg, tm=512, tn=512):
    s, d = hn.shape
    return pl.pallas_call(
        _merge_kernel,
        grid=(d // tn, s // tm),
        in_specs=[pl.BlockSpec((tm, d), lambda j, i: (i, 0)),
                  pl.BlockSpec((tm, ATTN_WIDTH), lambda j, i: (i, 0)),
                  pl.BlockSpec((tm, SGU_WIDTH), lambda j, i: (i, 0)),
                  pl.BlockSpec((d, tn), lambda j, i: (0, j)),
                  pl.BlockSpec((d, tn), lambda j, i: (0, j)),
                  pl.BlockSpec((1, tn), lambda j, i: (0, j)),
                  pl.BlockSpec((1, tn), lambda j, i: (0, j)),
                  pl.BlockSpec((ATTN_WIDTH, tn), lambda j, i: (0, j)),
                  pl.BlockSpec((SGU_WIDTH, tn), lambda j, i: (0, j))],
        out_specs=pl.BlockSpec((tm, tn), lambda j, i: (i, j)),
        out_shape=jax.ShapeDtypeStruct((s, d), BF16),
        compiler_params=_params(("parallel", "parallel")),
        name="gated_merge",
    )(hn, attn, sgu, wga, wgs, ba, bs, pa, pg)


def _out_kernel(x_ref, mg_ref, wo_ref, g_ref, h_ref, xn_ref):
    h = x_ref[...] + jnp.dot(mg_ref[...], wo_ref[...], preferred_element_type=F32)
    h_ref[...] = h
    ms = jnp.mean(h * h, axis=-1, keepdims=True)
    xn_ref[...] = (h * lax.rsqrt(ms + EPS) * g_ref[...]).astype(xn_ref.dtype)


def _out_proj(x, merged, wo, g, tm=256):
    s, d = x.shape
    return pl.pallas_call(
        _out_kernel,
        grid=(s // tm,),
        in_specs=[pl.BlockSpec((tm, d), lambda i: (i, 0)),
                  pl.BlockSpec((tm, d), lambda i: (i, 0)),
                  pl.BlockSpec((d, d), lambda i: (0, 0)),
                  pl.BlockSpec((1, d), lambda i: (0, 0))],
        out_specs=[pl.BlockSpec((tm, d), lambda i: (i, 0)),
                   pl.BlockSpec((tm, d), lambda i: (i, 0))],
        out_shape=[jax.ShapeDtypeStruct((s, d), F32),
                   jax.ShapeDtypeStruct((s, d), BF16)],
        compiler_params=_params(("parallel",)),
        name="out_proj_norm",
    )(x, merged, wo, g)


def _sorting_network(n):
    def merge(lo, hi, r):
        step = r * 2
        if step < hi - lo:
            yield from merge(lo, hi, step)
            yield from merge(lo + r, hi, step)
            yield from [(i, i + r) for i in range(lo + r, hi - r, step)]
        else:
            yield (lo, lo + r)

    def sort(lo, hi):
        if hi - lo >= 1:
            mid = lo + (hi - lo) // 2
            yield from sort(lo, mid)
            yield from sort(mid + 1, hi)
            yield from merge(lo, hi, 1)

    return tuple(sort(0, n - 1))


def _compare_exchange(x, i, j):
    a, b = x[i], x[j]
    if b is None:
        return
    if a is None:
        x[i], x[j] = b, None
        return
    x[i], x[j] = jnp.maximum(a, b), jnp.minimum(a, b)


def _sort_desc(x):
    for i, j in _sorting_network(len(x)):
        _compare_exchange(x, i, j)
    return x


def _bitonic_sort_desc(x):
    n = len(x)
    stride = n // 2
    while stride >= 1:
        for i in range(n):
            if i & stride == 0:
                _compare_exchange(x, i, i + stride)
        stride //= 2
    return x


def _max_of(vals):
    vals = [v for v in vals if v is not None]
    while len(vals) > 1:
        vals = [jnp.maximum(vals[i], vals[i + 1]) for i in range(0, len(vals) - 1, 2)] + \
               ([vals[-1]] if len(vals) % 2 else [])
    return vals[0] if vals else None


def _merge_with_sublane_neighbour(x, runner_up, shift):
    n = len(x)

    def roll(a):
        return None if a is None else pltpu.roll(a, shift, axis=0)

    y = [roll(a) for a in x]
    hi, lo = [], []
    for i in range(n):
        a, b = x[i], y[n - 1 - i]
        if a is None or b is None:
            hi.append(b if a is None else a)
            lo.append(None)
        else:
            hi.append(jnp.maximum(a, b))
            lo.append(jnp.minimum(a, b))
    return _bitonic_sort_desc(hi), _max_of(lo + [runner_up, roll(runner_up)])


def _rank_over_sublanes_and_tiles(tiles):
    x, runner_up = _sort_desc(list(tiles)), None
    for shift in (4, 2, 1):
        x, runner_up = _merge_with_sublane_neighbour(x, runner_up, shift)
    return x + [runner_up]


def _peer_route_kernel(xn_ref, wq_ref, k1_ref, k2_ref,
                       a1_ref, p1_ref, s2_ref, p2_ref, st_scr):
    n_lt = st_scr.shape[1]
    pq = jnp.dot(xn_ref[...], wq_ref[...], preferred_element_type=F32).astype(BF16)
    nt = (((1,), (1,)), ((), ()))
    for h in range(PEER_HEADS):
        for half, k_ref in enumerate((k1_ref, k2_ref)):
            lo = h * PEER_QUERY_DIM + half * PEER_HALF
            sc = lax.dot_general(k_ref[h], pq[:, lo:lo + PEER_HALF], nt, preferred_element_type=F32)
            for lt in range(n_lt):
                st_scr[2 * h + half, lt] = sc[:, lt * LANES:(lt + 1) * LANES]

    n_tiles = PEER_N_KEYS // SUBLANES
    sub = lax.broadcasted_iota(jnp.int32, (SUBLANES, LANES), 0)

    def spread(vals):
        out = vals[-1]
        for r in range(len(vals) - 2, -1, -1):
            out = jnp.where(sub == r, vals[r], out)
        return out

    def per_slab(idx, carry):
        h = idx // n_lt
        lt = idx % n_lt
        s1 = st_scr[2 * h, lt]
        s2 = st_scr[2 * h + 1, lt]
        v1 = _rank_over_sublanes_and_tiles([s1[t * SUBLANES:(t + 1) * SUBLANES] for t in range(n_tiles)])
        v2 = _rank_over_sublanes_and_tiles([s2[t * SUBLANES:(t + 1) * SUBLANES] for t in range(n_tiles)])
        v2_lo, v2_hi = spread(v2[0:SUBLANES]), spread(v2[SUBLANES:2 * SUBLANES])
        v1_hi = spread(v1[SUBLANES:2 * SUBLANES])
        corner = jnp.where(sub == 0, v1[0] + v2[PEER_TOPK],
                           jnp.where(sub == 1, v1[PEER_TOPK] + v2[0], -jnp.inf))
        cands = [v1[0] + v2_lo, v1[0] + v2_hi] + [v1[i] + v2_lo for i in range(1, SUBLANES)]
        cands += [v1_hi + v2[0], corner]
        top = _rank_over_sublanes_and_tiles(cands + [None] * (n_tiles - len(cands)))
        cut = 0.5 * (top[PEER_TOPK - 1] + top[PEER_TOPK])
        z = jnp.ones((SUBLANES, LANES), F32)
        for r in range(1, PEER_TOPK):
            z = z + jnp.exp(top[r] - top[0])
        reps = (n_tiles, 1)
        a1_ref[h, lt] = jnp.tile(cut, reps) - s1
        s2_ref[h, lt] = s2
        p1_ref[h, lt] = jnp.exp(s1 - jnp.tile(v1[0], reps))
        p2_ref[h, lt] = jnp.exp(s2 - jnp.tile(v2[0], reps)) * jnp.tile(1.0 / z, reps)
        return carry

    lax.fori_loop(0, PEER_HEADS * n_lt, per_slab, 0)


def _peer_route(xn, wq, k1, k2, tm=512):
    s, d = xn.shape
    n_lt = tm // LANES
    stat = jax.ShapeDtypeStruct((PEER_HEADS, s // LANES, PEER_N_KEYS, LANES), F32)
    stat_spec = pl.BlockSpec((PEER_HEADS, n_lt, PEER_N_KEYS, LANES), lambda i: (0, i, 0, 0))
    return pl.pallas_call(
        _peer_route_kernel,
        grid=(s // tm,),
        in_specs=[pl.BlockSpec((tm, d), lambda i: (i, 0)),
                  pl.BlockSpec((d, PEER_HEADS * PEER_QUERY_DIM), lambda i: (0, 0)),
                  pl.BlockSpec((PEER_HEADS, PEER_N_KEYS, PEER_HALF), lambda i: (0, 0, 0)),
                  pl.BlockSpec((PEER_HEADS, PEER_N_KEYS, PEER_HALF), lambda i: (0, 0, 0))],
        out_specs=[stat_spec, stat_spec, stat_spec, stat_spec],
        out_shape=[stat, stat, stat, stat],
        scratch_shapes=[pltpu.VMEM((2 * PEER_HEADS, n_lt, PEER_N_KEYS, LANES), F32)],
        compiler_params=_params(("parallel",)),
        name="peer_route",
    )(xn, wq, k1, k2)


def _peer_dense_kernel(xn_ref, h_ref, a1_ref, p1_ref, s2_ref, p2_ref, u_ref, v_ref, o_ref, at_scr):
    eb = pl.program_id(1)
    tt = xn_ref.shape[0]
    n_e1 = a1_ref.shape[2]
    e1_per_chain = n_e1 // N_CHAINS
    ce = e1_per_chain * PEER_N_KEYS

    @pl.when(eb == 0)
    def _():
        o_ref[...] = h_ref[...]

    xn = xn_ref[...]

    def pre_activations(c):
        return lax.dot_general(xn, u_ref[c * ce:(c + 1) * ce, :], (((1,), (1,)), ((), ())),
                               preferred_element_type=F32)

    st_next = pre_activations(0)
    for c in range(N_CHAINS):
        es = slice(c * ce, (c + 1) * ce)
        st = st_next
        if c + 1 < N_CHAINS:
            st_next = pre_activations(c + 1)
        for jj in range(e1_per_chain):
            j = c * e1_per_chain + jj
            for lt in range(tt // LANES):
                cols = slice(lt * LANES, (lt + 1) * LANES)
                sb = st[lt * LANES:(lt + 1) * LANES, jj * PEER_N_KEYS:(jj + 1) * PEER_N_KEYS]
                w = jnp.zeros((PEER_N_KEYS, LANES), F32)
                for h in range(PEER_HEADS):
                    keep = s2_ref[h, lt] >= a1_ref[h, lt, j:j + 1, :]
                    w = w + p1_ref[h, lt, j:j + 1, :] * jnp.where(keep, p2_ref[h, lt], 0.0)
                at_scr[cols, j * PEER_N_KEYS:(j + 1) * PEER_N_KEYS] = (_gelu(sb) * w.T).astype(at_scr.dtype)
        o_ref[...] += jnp.dot(at_scr[:, es], v_ref[es, :], preferred_element_type=F32)


def _peer_dense(xn, h1, a1, p1, s2, p2, u, v, tt=512, te=1024):
    s, d = xn.shape
    n_e1 = te // PEER_N_KEYS
    n_lt = tt // LANES
    return pl.pallas_call(
        _peer_dense_kernel,
        grid=(s // tt, PEER_N_EXPERTS // te),
        in_specs=[pl.BlockSpec((tt, d), lambda t, e: (t, 0)),
                  pl.BlockSpec((tt, d), lambda t, e: (t, 0)),
                  pl.BlockSpec((PEER_HEADS, n_lt, n_e1, LANES), lambda t, e: (0, t, e, 0)),
                  pl.BlockSpec((PEER_HEADS, n_lt, n_e1, LANES), lambda t, e: (0, t, e, 0)),
                  pl.BlockSpec((PEER_HEADS, n_lt, PEER_N_KEYS, LANES), lambda t, e: (0, t, 0, 0)),
                  pl.BlockSpec((PEER_HEADS, n_lt, PEER_N_KEYS, LANES), lambda t, e: (0, t, 0, 0)),
                  pl.BlockSpec((te, d), lambda t, e: (e, 0)),
                  pl.BlockSpec((te, d), lambda t, e: (e, 0))],
        out_specs=pl.BlockSpec((tt, d), lambda t, e: (t, 0)),
        out_shape=jax.ShapeDtypeStruct((s, d), F32),
        scratch_shapes=[pltpu.VMEM((tt, te), BF16)],
        compiler_params=_params(("parallel", "arbitrary")),
        name="peer_dense",
    )(xn, h1, a1, p1, s2, p2, u, v)


def _rope_tables(seq_len):
    rows = seq_len // GRID_W
    row = jnp.repeat(jnp.arange(rows, dtype=F32), GRID_W)
    col = jnp.tile(jnp.arange(GRID_W, dtype=F32), rows)
    n_pairs = HEAD_DIM // 4
    inv_freq = ROPE_THETA ** (-jnp.arange(n_pairs, dtype=F32) / n_pairs)
    ang = jnp.concatenate([row[:, None] * inv_freq, col[:, None] * inv_freq], axis=-1)
    cos, sin = jnp.cos(ang), jnp.sin(ang)
    return jnp.concatenate([cos, cos], axis=-1), jnp.concatenate([-sin, sin], axis=-1)


def kernel(x, norm_mix, w_in, b_gate, q_norm, k_norm, sgu_norm, w_sgu, b_sgu, w_attn_proj,
           w_sgu_proj, w_out, norm_ffn, w_peer_q, peer_k1, peer_k2, peer_u, peer_v):
    b, s, d = x.shape
    assert b == 1 and norm_mix.shape[0] == 1
    l = 0
    xs = x.reshape(s, d)

    pair_perm = np.concatenate([np.arange(0, HEAD_DIM, 2), np.arange(1, HEAD_DIM, 2)])
    q_cols = (np.arange(N_Q_HEADS)[:, None] * HEAD_DIM + pair_perm[None, :]).reshape(-1)
    k_cols = ATTN_WIDTH + (np.arange(N_KV_HEADS)[:, None] * HEAD_DIM + pair_perm[None, :]).reshape(-1)
    v_cols = ATTN_WIDTH + KV_WIDTH + np.arange(KV_WIDTH)
    w = w_in[l]
    o_su = ATTN_WIDTH + 2 * KV_WIDTH
    o_sv = o_su + SGU_WIDTH
    o_g = o_sv + SGU_WIDTH
    w_qkv = w[:, np.concatenate([q_cols, k_cols, v_cols])].astype(BF16)
    w_su = w[:, o_su:o_sv].astype(BF16)
    w_sv = w[:, o_sv:o_g].astype(BF16)
    w_ga = w[:, o_g:o_g + d].astype(BF16)
    w_gs = w[:, o_g + d:].astype(BF16)
    cos2, sin2 = _rope_tables(s)

    hn, q, k, v = _norm_qkv(xs, norm_mix[l].reshape(1, d), w_qkv, q_norm[l][pair_perm].reshape(1, HEAD_DIM),
                            k_norm[l][pair_perm].reshape(1, HEAD_DIM), cos2, sin2)
    attn = _attention(q, k, v)
    bias_s = jnp.broadcast_to(b_sgu[l][:, :, None], (SGU_GROUPS, SGU_CHUNK, 128))
    sgu = _sgu(hn, w_su, w_sv, sgu_norm[l].reshape(1, SGU_WIDTH), w_sgu[l].astype(BF16), bias_s)
    merged = _merge(hn, attn, sgu, w_ga, w_gs, b_gate[l][:d].reshape(1, d), b_gate[l][d:].reshape(1, d),
                    w_attn_proj[l].astype(BF16), w_sgu_proj[l].astype(BF16))
    h1, xn2 = _out_proj(xs, merged, w_out[l].astype(BF16), norm_ffn[l].reshape(1, d))
    a1, p1, s2, p2 = _peer_route(xn2, w_peer_q[l].astype(BF16), peer_k1[l].astype(BF16),
                                 peer_k2[l].astype(BF16))
    out = _peer_dense(xn2, h1, a1, p1, s2, p2, peer_u[l].astype(BF16), peer_v[l].astype(BF16))
    return out.reshape(b, s, d)
```

```python
import functools
import math

import jax
import jax.numpy as jnp
import numpy as np
from jax import lax
from jax.experimental import pallas as pl
from jax.experimental.pallas import tpu as pltpu

F32 = jnp.float32
BF16 = jnp.bfloat16

D_MODEL = 2048
GRID_W = 64
ROPE_THETA = 10000.0
N_Q_HEADS = 8
N_KV_HEADS = 2
HEAD_DIM = 128
GQA_GROUP = N_Q_HEADS // N_KV_HEADS
ATTN_WIDTH = N_Q_HEADS * HEAD_DIM
KV_WIDTH = N_KV_HEADS * HEAD_DIM
SGU_GROUPS = 8
SGU_CHUNK = 128
SGU_WIDTH = SGU_GROUPS * 128
PEER_HEADS = 8
PEER_N_KEYS = 128
PEER_N_EXPERTS = PEER_N_KEYS * PEER_N_KEYS
PEER_QUERY_DIM = 256
PEER_HALF = PEER_QUERY_DIM // 2
PEER_TOPK = 16
EPS = 1e-6

LANES = 128
SUBLANES = 8
VMEM_LIMIT = 56 * 1024 * 1024

N_RANK = PEER_TOPK + 1
ATTN_KEY_CHUNK = 2048
N_CHAINS = 4


def _gelu(x):
    c = math.sqrt(2.0 / math.pi)
    half = 0.5 * x
    return half + half * jnp.tanh(x * (c + (c * 0.044715) * (x * x)))


def _gelu_times_twice(x, w_half):
    c = math.sqrt(2.0 / math.pi)
    xw = x * w_half
    return xw + xw * jnp.tanh(x * (c + (c * 0.044715) * (x * x)))


def _sigmoid(x):
    return 1.0 / (1.0 + jnp.exp(-x))


def _params(sem, vmem=VMEM_LIMIT):
    return pltpu.CompilerParams(dimension_semantics=sem, vmem_limit_bytes=vmem)


def _qkv_kernel(x_ref, g_ref, w_ref, qg_ref, kg_ref, c_ref, s_ref, hn_ref, q_ref, k_ref, v_ref):
    x = x_ref[...]
    ms = jnp.mean(x * x, axis=-1, keepdims=True)
    hn = (x * lax.rsqrt(ms + EPS) * g_ref[...]).astype(hn_ref.dtype)
    hn_ref[...] = hn
    z = jnp.dot(hn, w_ref[...], preferred_element_type=F32)
    c = c_ref[...]
    s = s_ref[...]

    def norm_rope(xh, g):
        ms = jnp.mean(xh * xh, axis=-1, keepdims=True)
        y = xh * lax.rsqrt(ms + EPS) * g
        return y * c + pltpu.roll(y, HEAD_DIM // 2, axis=1) * s

    scale = math.log2(math.e) / math.sqrt(HEAD_DIM)
    for h in range(N_Q_HEADS):
        sl = slice(h * HEAD_DIM, (h + 1) * HEAD_DIM)
        q_ref[:, sl] = (norm_rope(z[:, sl], qg_ref[...]) * scale).astype(q_ref.dtype)
    ones = jnp.ones((z.shape[0], HEAD_DIM), v_ref.dtype)
    for h in range(N_KV_HEADS):
        sl = slice(h * HEAD_DIM, (h + 1) * HEAD_DIM)
        zs = slice(ATTN_WIDTH + h * HEAD_DIM, ATTN_WIDTH + (h + 1) * HEAD_DIM)
        k_ref[:, sl] = norm_rope(z[:, zs], kg_ref[...]).astype(k_ref.dtype)
        vs = slice(ATTN_WIDTH + KV_WIDTH + h * HEAD_DIM, ATTN_WIDTH + KV_WIDTH + (h + 1) * HEAD_DIM)
        v_ref[:, 2 * h * HEAD_DIM:(2 * h + 1) * HEAD_DIM] = z[:, vs].astype(v_ref.dtype)
        v_ref[:, (2 * h + 1) * HEAD_DIM:(2 * h + 2) * HEAD_DIM] = ones


def _norm_qkv(x, g, w_qkv, qg, kg, cos2, sin2, tm=512):
    s, d = x.shape
    n = w_qkv.shape[1]
    return pl.pallas_call(
        _qkv_kernel,
        grid=(s // tm,),
        in_specs=[pl.BlockSpec((tm, d), lambda i: (i, 0)),
                  pl.BlockSpec((1, d), lambda i: (0, 0)),
                  pl.BlockSpec((d, n), lambda i: (0, 0)),
                  pl.BlockSpec((1, HEAD_DIM), lambda i: (0, 0)),
                  pl.BlockSpec((1, HEAD_DIM), lambda i: (0, 0)),
                  pl.BlockSpec((tm, HEAD_DIM), lambda i: (i, 0)),
                  pl.BlockSpec((tm, HEAD_DIM), lambda i: (i, 0))],
        out_specs=[pl.BlockSpec((tm, d), lambda i: (i, 0)),
                   pl.BlockSpec((tm, ATTN_WIDTH), lambda i: (i, 0)),
                   pl.BlockSpec((tm, KV_WIDTH), lambda i: (i, 0)),
                   pl.BlockSpec((tm, 2 * KV_WIDTH), lambda i: (i, 0))],
        out_shape=[jax.ShapeDtypeStruct((s, d), BF16),
                   jax.ShapeDtypeStruct((s, ATTN_WIDTH), BF16),
                   jax.ShapeDtypeStruct((s, KV_WIDTH), BF16),
                   jax.ShapeDtypeStruct((s, 2 * KV_WIDTH), BF16)],
        compiler_params=_params(("parallel",)),
        name="norm_qkv_rope",
    )(x, g, w_qkv, qg, kg, cos2, sin2)


def _key_chunks(tk):
    edge = ATTN_KEY_CHUNK // 4
    if tk < 2 * ATTN_KEY_CHUNK:
        return [min(tk, ATTN_KEY_CHUNK)] * max(1, tk // ATTN_KEY_CHUNK)
    n_full = tk // ATTN_KEY_CHUNK - 1
    return [edge] + [ATTN_KEY_CHUNK] * n_full + [ATTN_KEY_CHUNK - 2 * edge, edge]


def _attn_kernel(q_ref, k_ref, v_ref, o_ref, m_ref, acc_ref):
    ki = pl.program_id(2)
    tk = k_ref.shape[0]

    @pl.when(ki == 0)
    def _():
        m_ref[...] = jnp.full(m_ref.shape, -jnp.inf, F32)
        acc_ref[...] = jnp.zeros(acc_ref.shape, F32)

    lo = 0
    for kc in _key_chunks(tk):
        k = k_ref[lo:lo + kc, :]
        v = v_ref[lo:lo + kc, :]
        lo += kc
        for g in range(GQA_GROUP):
            q = q_ref[:, g * HEAD_DIM:(g + 1) * HEAD_DIM]
            s = lax.dot_general(q, k, (((1,), (1,)), ((), ())), preferred_element_type=F32)
            m_prev = m_ref[g]
            m_new = jnp.maximum(m_prev, jnp.max(s, axis=1, keepdims=True))
            alpha = jnp.exp2(m_prev - m_new)
            p = jnp.exp2(s - jnp.tile(m_new, (1, kc // LANES)))
            acc_ref[g] = jnp.tile(alpha, (1, 2)) * acc_ref[g] + jnp.dot(p.astype(v.dtype), v,
                                                                         preferred_element_type=F32)
            m_ref[g] = m_new

    @pl.when(ki == pl.num_programs(2) - 1)
    def _():
        for g in range(GQA_GROUP):
            acc = acc_ref[g]
            o_ref[:, g * HEAD_DIM:(g + 1) * HEAD_DIM] = (acc[:, :HEAD_DIM] / acc[:, HEAD_DIM:]).astype(o_ref.dtype)


def _attention(q, k, v, tq=512, tk=8192):
    s = q.shape[0]
    gw = GQA_GROUP * HEAD_DIM
    return pl.pallas_call(
        _attn_kernel,
        grid=(N_KV_HEADS, s // tq, s // tk),
        in_specs=[pl.BlockSpec((tq, gw), lambda h, i, j: (i, h)),
                  pl.BlockSpec((tk, HEAD_DIM), lambda h, i, j: (j, h)),
                  pl.BlockSpec((tk, 2 * HEAD_DIM), lambda h, i, j: (j, h))],
        out_specs=pl.BlockSpec((tq, gw), lambda h, i, j: (i, h)),
        out_shape=jax.ShapeDtypeStruct((s, ATTN_WIDTH), BF16),
        scratch_shapes=[pltpu.VMEM((GQA_GROUP, tq, LANES), F32),
                        pltpu.VMEM((GQA_GROUP, tq, 2 * HEAD_DIM), F32)],
        compiler_params=_params(("parallel", "parallel", "arbitrary")),
        name="flash_attention",
    )(q, k, v)


def _sgu_kernel(hn_ref, wu_ref, wv_ref, gn_ref, ws_ref, bs_ref, o_ref):
    hn = hn_ref[...]
    tm = hn.shape[0]
    u = _gelu(jnp.dot(hn, wu_ref[...], preferred_element_type=F32))
    v = _gelu(jnp.dot(hn, wv_ref[...], preferred_element_type=F32))
    ms = jnp.mean(v * v, axis=-1, keepdims=True)
    vn = (v * lax.rsqrt(ms + EPS) * gn_ref[...]).astype(BF16)
    for c in range(tm // SGU_CHUNK):
        rows = slice(c * SGU_CHUNK, (c + 1) * SGU_CHUNK)
        for g in range(SGU_GROUPS):
            cols = slice(g * 128, (g + 1) * 128)
            mixed = jnp.dot(ws_ref[g], vn[rows, cols], preferred_element_type=F32) + bs_ref[g]
            o_ref[rows, cols] = (u[rows, cols] * mixed).astype(o_ref.dtype)


def _sgu(hn, wu, wv, gn, ws, bs, tm=512):
    s, d = hn.shape
    return pl.pallas_call(
        _sgu_kernel,
        grid=(s // tm,),
        in_specs=[pl.BlockSpec((tm, d), lambda i: (i, 0)),
                  pl.BlockSpec((d, SGU_WIDTH), lambda i: (0, 0)),
                  pl.BlockSpec((d, SGU_WIDTH), lambda i: (0, 0)),
                  pl.BlockSpec((1, SGU_WIDTH), lambda i: (0, 0)),
                  pl.BlockSpec((SGU_GROUPS, SGU_CHUNK, SGU_CHUNK), lambda i: (0, 0, 0)),
                  pl.BlockSpec((SGU_GROUPS, SGU_CHUNK, 128), lambda i: (0, 0, 0))],
        out_specs=pl.BlockSpec((tm, SGU_WIDTH), lambda i: (i, 0)),
        out_shape=jax.ShapeDtypeStruct((s, SGU_WIDTH), BF16),
        compiler_params=_params(("parallel",)),
        name="sgu",
    )(hn, wu, wv, gn, ws, bs)


def _merge_kernel(hn_ref, a_ref, s_ref, wga_ref, wgs_ref, ba_ref, bs_ref, pa_ref, pg_ref, o_ref):
    hn = hn_ref[...]
    ga = _sigmoid(jnp.dot(hn, wga_ref[...], preferred_element_type=F32) + ba_ref[...])
    gs = _sigmoid(jnp.dot(hn, wgs_ref[...], preferred_element_type=F32) + bs_ref[...])
    ya = jnp.dot(a_ref[...], pa_ref[...], preferred_element_type=F32)
    ys = jnp.dot(s_ref[...], pg_ref[...], preferred_element_type=F32)
    o_ref[...] = (ga * ya + gs * ys).astype(o_ref.dtype)


def _merge(hn, attn, sgu, wga, wgs, ba, bs, pa, pg, tm=512, tn=512):
    s, d = hn.shape
    return pl.pallas_call(
        _merge_kernel,
        grid=(d // tn, s // tm),
        in_specs=[pl.BlockSpec((tm, d), lambda j, i: (i, 0)),
                  pl.BlockSpec((tm, ATTN_WIDTH), lambda j, i: (i, 0)),
                  pl.BlockSpec((tm, SGU_WIDTH), lambda j, i: (i, 0)),
                  pl.BlockSpec((d, tn), lambda j, i: (0, j)),
                  pl.BlockSpec((d, tn), lambda j, i: (0, j)),
                  pl.BlockSpec((1, tn), lambda j, i: (0, j)),
                  pl.BlockSpec((1, tn), lambda j, i: (0, j)),
                  pl.BlockSpec((ATTN_WIDTH, tn), lambda j, i: (0, j)),
                  pl.BlockSpec((SGU_WIDTH, tn), lambda j, i: (0, j))],
        out_specs=pl.BlockSpec((tm, tn), lambda j, i: (i, j)),
        out_shape=jax.ShapeDtypeStruct((s, d), BF16),
        compiler_params=_params(("parallel", "parallel")),
        name="gated_merge",
    )(hn, attn, sgu, wga, wgs, ba, bs, pa, pg)


def _out_kernel(x_ref, mg_ref, wo_ref, g_ref, h_ref, xn_ref):
    h = x_ref[...] + jnp.dot(mg_ref[...], wo_ref[...], preferred_element_type=F32)
    h_ref[...] = h
    ms = jnp.mean(h * h, axis=-1, keepdims=True)
    xn_ref[...] = (h * lax.rsqrt(ms + EPS) * g_ref[...]).astype(xn_ref.dtype)


def _out_proj(x, merged, wo, g, tm=512):
    s, d = x.shape
    return pl.pallas_call(
        _out_kernel,
        grid=(s // tm,),
        in_specs=[pl.BlockSpec((tm, d), lambda i: (i, 0)),
                  pl.BlockSpec((tm, d), lambda i: (i, 0)),
                  pl.BlockSpec((d, d), lambda i: (0, 0)),
                  pl.BlockSpec((1, d), lambda i: (0, 0))],
        out_specs=[pl.BlockSpec((tm, d), lambda i: (i, 0)),
                   pl.BlockSpec((tm, d), lambda i: (i, 0))],
        out_shape=[jax.ShapeDtypeStruct((s, d), F32),
                   jax.ShapeDtypeStruct((s, d), BF16)],
        compiler_params=_params(("parallel",)),
        name="out_proj_norm",
    )(x, merged, wo, g)


def _sorting_network(n):
    def merge(lo, hi, r):
        step = r * 2
        if step < hi - lo:
            yield from merge(lo, hi, step)
            yield from merge(lo + r, hi, step)
            yield from [(i, i + r) for i in range(lo + r, hi - r, step)]
        else:
            yield (lo, lo + r)

    def sort(lo, hi):
        if hi - lo >= 1:
            mid = lo + (hi - lo) // 2
            yield from sort(lo, mid)
            yield from sort(mid + 1, hi)
            yield from merge(lo, hi, 1)

    return tuple(sort(0, n - 1))


def _compare_exchange(x, i, j):
    a, b = x[i], x[j]
    if b is None:
        return
    if a is None:
        x[i], x[j] = b, None
        return
    x[i], x[j] = jnp.maximum(a, b), jnp.minimum(a, b)


def _sort_desc(x):
    for i, j in _sorting_network(len(x)):
        _compare_exchange(x, i, j)
    return x


def _bitonic_sort_desc(x):
    n = len(x)
    stride = n // 2
    while stride >= 1:
        for i in range(n):
            if i & stride == 0:
                _compare_exchange(x, i, i + stride)
        stride //= 2
    return x


def _max_of(vals):
    vals = [v for v in vals if v is not None]
    while len(vals) > 1:
        vals = [jnp.maximum(vals[i], vals[i + 1]) for i in range(0, len(vals) - 1, 2)] + \
               ([vals[-1]] if len(vals) % 2 else [])
    return vals[0] if vals else None


def _merge_with_sublane_neighbour(x, runner_up, shift):
    n = len(x)

    def roll(a):
        return None if a is None else pltpu.roll(a, shift, axis=0)

    y = [roll(a) for a in x]
    hi, lo = [], []
    for i in range(n):
        a, b = x[i], y[n - 1 - i]
        if a is None or b is None:
            hi.append(b if a is None else a)
            lo.append(None)
        else:
            hi.append(jnp.maximum(a, b))
            lo.append(jnp.minimum(a, b))
    return _bitonic_sort_desc(hi), _max_of(lo + [runner_up, roll(runner_up)])


def _rank_over_sublanes_and_tiles(tiles):
    x, runner_up = _sort_desc(list(tiles)), None
    for shift in (4, 2, 1):
        x, runner_up = _merge_with_sublane_neighbour(x, runner_up, shift)
    return x + [runner_up]


def _peer_route_kernel(xn_ref, wq_ref, k1_ref, k2_ref,
                       a1_ref, p1_ref, s2_ref, p2_ref, st_scr):
    n_lt = st_scr.shape[1]
    pq = jnp.dot(xn_ref[...], wq_ref[...], preferred_element_type=F32).astype(BF16)
    nt = (((1,), (1,)), ((), ()))
    for h in range(PEER_HEADS):
        for half, k_ref in enumerate((k1_ref, k2_ref)):
            lo = h * PEER_QUERY_DIM + half * PEER_HALF
            sc = lax.dot_general(k_ref[h], pq[:, lo:lo + PEER_HALF], nt, preferred_element_type=F32)
            for lt in range(n_lt):
                st_scr[2 * h + half, lt] = sc[:, lt * LANES:(lt + 1) * LANES]

    n_tiles = PEER_N_KEYS // SUBLANES
    sub = lax.broadcasted_iota(jnp.int32, (SUBLANES, LANES), 0)

    def spread(vals):
        out = vals[-1]
        for r in range(len(vals) - 2, -1, -1):
            out = jnp.where(sub == r, vals[r], out)
        return out

    def per_slab(idx, carry):
        h = idx // n_lt
        lt = idx % n_lt
        s1 = st_scr[2 * h, lt]
        s2 = st_scr[2 * h + 1, lt]
        v1 = _rank_over_sublanes_and_tiles([s1[t * SUBLANES:(t + 1) * SUBLANES] for t in range(n_tiles)])
        v2 = _rank_over_sublanes_and_tiles([s2[t * SUBLANES:(t + 1) * SUBLANES] for t in range(n_tiles)])
        v2_lo, v2_hi = spread(v2[0:SUBLANES]), spread(v2[SUBLANES:2 * SUBLANES])
        v1_hi = spread(v1[SUBLANES:2 * SUBLANES])
        corner = jnp.where(sub == 0, v1[0] + v2[PEER_TOPK],
                           jnp.where(sub == 1, v1[PEER_TOPK] + v2[0], -jnp.inf))
        cands = [v1[0] + v2_lo, v1[0] + v2_hi] + [v1[i] + v2_lo for i in range(1, SUBLANES)]
        cands += [v1_hi + v2[0], corner]
        top = _rank_over_sublanes_and_tiles(cands + [None] * (n_tiles - len(cands)))
        cut = 0.5 * (top[PEER_TOPK - 1] + top[PEER_TOPK])
        z = jnp.ones((SUBLANES, LANES), F32)
        for r in range(1, PEER_TOPK):
            z = z + jnp.exp(top[r] - top[0])
        reps = (n_tiles, 1)
        a1_ref[h, lt] = jnp.tile(cut, reps) - s1
        s2_ref[h, lt] = s2
        p1_ref[h, lt] = jnp.exp(s1 - jnp.tile(v1[0], reps))
        p2_ref[h, lt] = jnp.exp(s2 - jnp.tile(v2[0], reps)) * jnp.tile(0.5 / z, reps)
        return carry

    lax.fori_loop(0, PEER_HEADS * n_lt, per_slab, 0)


def _peer_route(xn, wq, k1, k2, tm=512):
    s, d = xn.shape
    n_lt = tm // LANES
    stat = jax.ShapeDtypeStruct((PEER_HEADS, s // LANES, PEER_N_KEYS, LANES), F32)
    stat_spec = pl.BlockSpec((PEER_HEADS, n_lt, PEER_N_KEYS, LANES), lambda i: (0, i, 0, 0))
    return pl.pallas_call(
        _peer_route_kernel,
        grid=(s // tm,),
        in_specs=[pl.BlockSpec((tm, d), lambda i: (i, 0)),
                  pl.BlockSpec((d, PEER_HEADS * PEER_QUERY_DIM), lambda i: (0, 0)),
                  pl.BlockSpec((PEER_HEADS, PEER_N_KEYS, PEER_HALF), lambda i: (0, 0, 0)),
                  pl.BlockSpec((PEER_HEADS, PEER_N_KEYS, PEER_HALF), lambda i: (0, 0, 0))],
        out_specs=[stat_spec, stat_spec, stat_spec, stat_spec],
        out_shape=[stat, stat, stat, stat],
        scratch_shapes=[pltpu.VMEM((2 * PEER_HEADS, n_lt, PEER_N_KEYS, LANES), F32)],
        compiler_params=_params(("parallel",)),
        name="peer_route",
    )(xn, wq, k1, k2)


def _peer_dense_kernel(xn_ref, h_ref, a1_ref, p1_ref, s2_ref, p2_ref, u_ref, v_ref, o_ref, at_scr):
    eb = pl.program_id(1)
    tt = xn_ref.shape[0]
    n_e1 = a1_ref.shape[2]
    e1_per_chain = n_e1 // N_CHAINS
    ce = e1_per_chain * PEER_N_KEYS

    @pl.when(eb == 0)
    def _():
        o_ref[...] = h_ref[...]

    xn = xn_ref[...]

    def pre_activations(c):
        return lax.dot_general(xn, u_ref[c * ce:(c + 1) * ce, :], (((1,), (1,)), ((), ())),
                               preferred_element_type=F32)

    st_next = pre_activations(0)
    for c in range(N_CHAINS):
        es = slice(c * ce, (c + 1) * ce)
        st = st_next
        if c + 1 < N_CHAINS:
            st_next = pre_activations(c + 1)
        for jj in range(e1_per_chain):
            j = c * e1_per_chain + jj
            for lt in range(tt // LANES):
                cols = slice(lt * LANES, (lt + 1) * LANES)
                sb = st[lt * LANES:(lt + 1) * LANES, jj * PEER_N_KEYS:(jj + 1) * PEER_N_KEYS]
                w = jnp.zeros((PEER_N_KEYS, LANES), F32)
                for h in range(PEER_HEADS):
                    keep = s2_ref[h, lt] >= a1_ref[h, lt, j:j + 1, :]
                    w = w + p1_ref[h, lt, j:j + 1, :] * jnp.where(keep, p2_ref[h, lt], 0.0)
                at_scr[cols, j * PEER_N_KEYS:(j + 1) * PEER_N_KEYS] = _gelu_times_twice(sb, w.T).astype(at_scr.dtype)
        o_ref[...] += jnp.dot(at_scr[:, es], v_ref[es, :], preferred_element_type=F32)


def _peer_dense(xn, h1, a1, p1, s2, p2, u, v, tt=512, te=1024):
    s, d = xn.shape
    n_e1 = te // PEER_N_KEYS
    n_lt = tt // LANES
    return pl.pallas_call(
        _peer_dense_kernel,
        grid=(s // tt, PEER_N_EXPERTS // te),
        in_specs=[pl.BlockSpec((tt, d), lambda t, e: (t, 0)),
                  pl.BlockSpec((tt, d), lambda t, e: (t, 0)),
                  pl.BlockSpec((PEER_HEADS, n_lt, n_e1, LANES), lambda t, e: (0, t, e, 0)),
                  pl.BlockSpec((PEER_HEADS, n_lt, n_e1, LANES), lambda t, e: (0, t, e, 0)),
                  pl.BlockSpec((PEER_HEADS, n_lt, PEER_N_KEYS, LANES), lambda t, e: (0, t, 0, 0)),
                  pl.BlockSpec((PEER_HEADS, n_lt, PEER_N_KEYS, LANES), lambda t, e: (0, t, 0, 0)),
                  pl.BlockSpec((te, d), lambda t, e: (e, 0)),
                  pl.BlockSpec((te, d), lambda t, e: (e, 0))],
        out_specs=pl.BlockSpec((tt, d), lambda t, e: (t, 0)),
        out_shape=jax.ShapeDtypeStruct((s, d), F32),
        scratch_shapes=[pltpu.VMEM((tt, te), BF16)],
        compiler_params=_params(("parallel", "arbitrary")),
        name="peer_dense",
    )(xn, h1, a1, p1, s2, p2, u, v)


def _rope_tables(seq_len):
    rows = seq_len // GRID_W
    row = jnp.repeat(jnp.arange(rows, dtype=F32), GRID_W)
    col = jnp.tile(jnp.arange(GRID_W, dtype=F32), rows)
    n_pairs = HEAD_DIM // 4
    inv_freq = ROPE_THETA ** (-jnp.arange(n_pairs, dtype=F32) / n_pairs)
    ang = jnp.concatenate([row[:, None] * inv_freq, col[:, None] * inv_freq], axis=-1)
    cos, sin = jnp.cos(ang), jnp.sin(ang)
    return jnp.concatenate([cos, cos], axis=-1), jnp.concatenate([-sin, sin], axis=-1)


def kernel(x, norm_mix, w_in, b_gate, q_norm, k_norm, sgu_norm, w_sgu, b_sgu, w_attn_proj,
           w_sgu_proj, w_out, norm_ffn, w_peer_q, peer_k1, peer_k2, peer_u, peer_v):
    b, s, d = x.shape
    assert b == 1 and norm_mix.shape[0] == 1
    l = 0
    xs = x.reshape(s, d)

    pair_perm = np.concatenate([np.arange(0, HEAD_DIM, 2), np.arange(1, HEAD_DIM, 2)])
    q_cols = (np.arange(N_Q_HEADS)[:, None] * HEAD_DIM + pair_perm[None, :]).reshape(-1)
    k_cols = ATTN_WIDTH + (np.arange(N_KV_HEADS)[:, None] * HEAD_DIM + pair_perm[None, :]).reshape(-1)
    v_cols = ATTN_WIDTH + KV_WIDTH + np.arange(KV_WIDTH)
    w = w_in[l]
    o_su = ATTN_WIDTH + 2 * KV_WIDTH
    o_sv = o_su + SGU_WIDTH
    o_g = o_sv + SGU_WIDTH
    w_qkv = w[:, np.concatenate([q_cols, k_cols, v_cols])].astype(BF16)
    w_su = w[:, o_su:o_sv].astype(BF16)
    w_sv = w[:, o_sv:o_g].astype(BF16)
    w_ga = w[:, o_g:o_g + d].astype(BF16)
    w_gs = w[:, o_g + d:].astype(BF16)
    cos2, sin2 = _rope_tables(s)

    hn, q, k, v = _norm_qkv(xs, norm_mix[l].reshape(1, d), w_qkv, q_norm[l][pair_perm].reshape(1, HEAD_DIM),
                            k_norm[l][pair_perm].reshape(1, HEAD_DIM), cos2, sin2)
    attn = _attention(q, k, v)
    bias_s = jnp.broadcast_to(b_sgu[l][:, :, None], (SGU_GROUPS, SGU_CHUNK, 128))
    sgu = _sgu(hn, w_su, w_sv, sgu_norm[l].reshape(1, SGU_WIDTH), w_sgu[l].astype(BF16), bias_s)
    merged = _merge(hn, attn, sgu, w_ga, w_gs, b_gate[l][:d].reshape(1, d), b_gate[l][d:].reshape(1, d),
                    w_attn_proj[l].astype(BF16), w_sgu_proj[l].astype(BF16))
    h1, xn2 = _out_proj(xs, merged, w_out[l].astype(BF16), norm_ffn[l].reshape(1, d))
    a1, p1, s2, p2 = _peer_route(xn2, w_peer_q[l].astype(BF16), peer_k1[l].astype(BF16),
                                 peer_k2[l].astype(BF16))
    out = _peer_dense(xn2, h1, a1, p1, s2, p2, peer_u[l].astype(BF16), peer_v[l].astype(BF16))
    return out.reshape(b, s, d)
```

```python
import math

import jax
import jax.numpy as jnp
import numpy as np
from jax import lax
from jax.experimental import pallas as pl
from jax.experimental.pallas import tpu as pltpu

F32 = jnp.float32
BF16 = jnp.bfloat16

GRID_W = 64
ROPE_THETA = 10000.0
N_Q_HEADS = 8
N_KV_HEADS = 2
HEAD_DIM = 128
GQA_GROUP = N_Q_HEADS // N_KV_HEADS
ATTN_WIDTH = N_Q_HEADS * HEAD_DIM
KV_WIDTH = N_KV_HEADS * HEAD_DIM
SGU_GROUPS = 8
SGU_CHUNK = 128
SGU_WIDTH = SGU_GROUPS * 128
PEER_HEADS = 8
PEER_N_KEYS = 128
PEER_N_EXPERTS = PEER_N_KEYS * PEER_N_KEYS
PEER_QUERY_DIM = 256
PEER_HALF = PEER_QUERY_DIM // 2
PEER_TOPK = 16
EPS = 1e-6

LANES = 128
SUBLANES = 8
VMEM_LIMIT = 56 * 1024 * 1024

N_RANK = PEER_TOPK + 1
ATTN_KEY_CHUNK = 2048
N_CHAINS = 4


def _gelu(x):
    c = math.sqrt(2.0 / math.pi)
    half = 0.5 * x
    return half + half * jnp.tanh(x * (c + (c * 0.044715) * (x * x)))


def _gelu_times_twice(x, w_half):
    c = math.sqrt(2.0 / math.pi)
    xw = x * w_half
    return xw + xw * jnp.tanh(x * (c + (c * 0.044715) * (x * x)))


def _sigmoid(x):
    return 1.0 / (1.0 + jnp.exp(-x))


def _params(sem, vmem=VMEM_LIMIT):
    return pltpu.CompilerParams(dimension_semantics=sem, vmem_limit_bytes=vmem)


def _qkv_kernel(x_ref, g_ref, w_ref, qg_ref, kg_ref, c_ref, s_ref, hn_ref, q_ref, k_ref, v_ref):
    x = x_ref[...]
    ms = jnp.mean(x * x, axis=-1, keepdims=True)
    hn = (x * lax.rsqrt(ms + EPS) * g_ref[...]).astype(hn_ref.dtype)
    hn_ref[...] = hn
    z = jnp.dot(hn, w_ref[...], preferred_element_type=F32)
    c = c_ref[...]
    s = s_ref[...]

    def norm_rope(xh, g):
        ms = jnp.mean(xh * xh, axis=-1, keepdims=True)
        y = xh * lax.rsqrt(ms + EPS) * g
        return y * c + pltpu.roll(y, HEAD_DIM // 2, axis=1) * s

    scale = math.log2(math.e) / math.sqrt(HEAD_DIM)
    for h in range(N_Q_HEADS):
        sl = slice(h * HEAD_DIM, (h + 1) * HEAD_DIM)
        q_ref[:, sl] = (norm_rope(z[:, sl], qg_ref[...]) * scale).astype(q_ref.dtype)
    ones = jnp.ones((z.shape[0], HEAD_DIM), v_ref.dtype)
    for h in range(N_KV_HEADS):
        sl = slice(h * HEAD_DIM, (h + 1) * HEAD_DIM)
        zs = slice(ATTN_WIDTH + h * HEAD_DIM, ATTN_WIDTH + (h + 1) * HEAD_DIM)
        k_ref[:, sl] = norm_rope(z[:, zs], kg_ref[...]).astype(k_ref.dtype)
        vs = slice(ATTN_WIDTH + KV_WIDTH + h * HEAD_DIM, ATTN_WIDTH + KV_WIDTH + (h + 1) * HEAD_DIM)
        v_ref[:, 2 * h * HEAD_DIM:(2 * h + 1) * HEAD_DIM] = z[:, vs].astype(v_ref.dtype)
        v_ref[:, (2 * h + 1) * HEAD_DIM:(2 * h + 2) * HEAD_DIM] = ones


def _norm_qkv(x, g, w_qkv, qg, kg, cos2, sin2, tm=512):
    s, d = x.shape
    n = w_qkv.shape[1]
    return pl.pallas_call(
        _qkv_kernel,
        grid=(s // tm,),
        in_specs=[pl.BlockSpec((tm, d), lambda i: (i, 0)),
                  pl.BlockSpec((1, d), lambda i: (0, 0)),
                  pl.BlockSpec((d, n), lambda i: (0, 0)),
                  pl.BlockSpec((1, HEAD_DIM), lambda i: (0, 0)),
                  pl.BlockSpec((1, HEAD_DIM), lambda i: (0, 0)),
                  pl.BlockSpec((tm, HEAD_DIM), lambda i: (i, 0)),
                  pl.BlockSpec((tm, HEAD_DIM), lambda i: (i, 0))],
        out_specs=[pl.BlockSpec((tm, d), lambda i: (i, 0)),
                   pl.BlockSpec((tm, ATTN_WIDTH), lambda i: (i, 0)),
                   pl.BlockSpec((tm, KV_WIDTH), lambda i: (i, 0)),
                   pl.BlockSpec((tm, 2 * KV_WIDTH), lambda i: (i, 0))],
        out_shape=[jax.ShapeDtypeStruct((s, d), BF16),
                   jax.ShapeDtypeStruct((s, ATTN_WIDTH), BF16),
                   jax.ShapeDtypeStruct((s, KV_WIDTH), BF16),
                   jax.ShapeDtypeStruct((s, 2 * KV_WIDTH), BF16)],
        compiler_params=_params(("parallel",)),
        name="norm_qkv_rope",
    )(x, g, w_qkv, qg, kg, cos2, sin2)


def _key_chunks(tk):
    edge = ATTN_KEY_CHUNK // 4
    if tk < 2 * ATTN_KEY_CHUNK:
        return [min(tk, ATTN_KEY_CHUNK)] * max(1, tk // ATTN_KEY_CHUNK)
    n_full = tk // ATTN_KEY_CHUNK - 1
    return [edge] + [ATTN_KEY_CHUNK] * n_full + [ATTN_KEY_CHUNK - 2 * edge, edge]


def _attn_kernel(q_ref, k_ref, v_ref, o_ref, m_ref, acc_ref):
    ki = pl.program_id(2)
    tk = k_ref.shape[0]

    @pl.when(ki == 0)
    def _():
        m_ref[...] = jnp.full(m_ref.shape, -jnp.inf, F32)
        acc_ref[...] = jnp.zeros(acc_ref.shape, F32)

    lo = 0
    for kc in _key_chunks(tk):
        k = k_ref[lo:lo + kc, :]
        v = v_ref[lo:lo + kc, :]
        lo += kc
        for g in range(GQA_GROUP):
            q = q_ref[:, g * HEAD_DIM:(g + 1) * HEAD_DIM]
            s = lax.dot_general(q, k, (((1,), (1,)), ((), ())), preferred_element_type=F32)
            m_prev = m_ref[g]
            m_new = jnp.maximum(m_prev, jnp.max(s, axis=1, keepdims=True))
            alpha = jnp.exp2(m_prev - m_new)
            p = jnp.exp2(s - jnp.tile(m_new, (1, kc // LANES)))
            acc_ref[g] = jnp.tile(alpha, (1, 2)) * acc_ref[g] + jnp.dot(p.astype(v.dtype), v,
                                                                         preferred_element_type=F32)
            m_ref[g] = m_new

    @pl.when(ki == pl.num_programs(2) - 1)
    def _():
        for g in range(GQA_GROUP):
            acc = acc_ref[g]
            o_ref[:, g * HEAD_DIM:(g + 1) * HEAD_DIM] = (acc[:, :HEAD_DIM] / acc[:, HEAD_DIM:]).astype(o_ref.dtype)


def _attention(q, k, v, tq=512, tk=8192):
    s = q.shape[0]
    gw = GQA_GROUP * HEAD_DIM
    return pl.pallas_call(
        _attn_kernel,
        grid=(N_KV_HEADS, s // tq, s // tk),
        in_specs=[pl.BlockSpec((tq, gw), lambda h, i, j: (i, h)),
                  pl.BlockSpec((tk, HEAD_DIM), lambda h, i, j: (j, h)),
                  pl.BlockSpec((tk, 2 * HEAD_DIM), lambda h, i, j: (j, h))],
        out_specs=pl.BlockSpec((tq, gw), lambda h, i, j: (i, h)),
        out_shape=jax.ShapeDtypeStruct((s, ATTN_WIDTH), BF16),
        scratch_shapes=[pltpu.VMEM((GQA_GROUP, tq, LANES), F32),
                        pltpu.VMEM((GQA_GROUP, tq, 2 * HEAD_DIM), F32)],
        compiler_params=_params(("parallel", "parallel", "arbitrary")),
        name="flash_attention",
    )(q, k, v)


def _sgu_kernel(hn_ref, wu_ref, wv_ref, gn_ref, ws_ref, bs_ref, o_ref):
    hn = hn_ref[...]
    tm = hn.shape[0]
    u = _gelu(jnp.dot(hn, wu_ref[...], preferred_element_type=F32))
    v = _gelu(jnp.dot(hn, wv_ref[...], preferred_element_type=F32))
    ms = jnp.mean(v * v, axis=-1, keepdims=True)
    vn = (v * lax.rsqrt(ms + EPS) * gn_ref[...]).astype(BF16)
    for c in range(tm // SGU_CHUNK):
        rows = slice(c * SGU_CHUNK, (c + 1) * SGU_CHUNK)
        for g in range(SGU_GROUPS):
            cols = slice(g * 128, (g + 1) * 128)
            mixed = jnp.dot(ws_ref[g], vn[rows, cols], preferred_element_type=F32) + bs_ref[g]
            o_ref[rows, cols] = (u[rows, cols] * mixed).astype(o_ref.dtype)


def _sgu(hn, wu, wv, gn, ws, bs, tm=512):
    s, d = hn.shape
    return pl.pallas_call(
        _sgu_kernel,
        grid=(s // tm,),
        in_specs=[pl.BlockSpec((tm, d), lambda i: (i, 0)),
                  pl.BlockSpec((d, SGU_WIDTH), lambda i: (0, 0)),
                  pl.BlockSpec((d, SGU_WIDTH), lambda i: (0, 0)),
                  pl.BlockSpec((1, SGU_WIDTH), lambda i: (0, 0)),
                  pl.BlockSpec((SGU_GROUPS, SGU_CHUNK, SGU_CHUNK), lambda i: (0, 0, 0)),
                  pl.BlockSpec((SGU_GROUPS, SGU_CHUNK, 128), lambda i: (0, 0, 0))],
        out_specs=pl.BlockSpec((tm, SGU_WIDTH), lambda i: (i, 0)),
        out_shape=jax.ShapeDtypeStruct((s, SGU_WIDTH), BF16),
        compiler_params=_params(("parallel",)),
        name="sgu",
    )(hn, wu, wv, gn, ws, bs)


def _merge_kernel(hn_ref, a_ref, s_ref, wga_ref, wgs_ref, ba_ref, bs_ref, pa_ref, pg_ref, o_ref):
    hn = hn_ref[...]
    ga = _sigmoid(jnp.dot(hn, wga_ref[...], preferred_element_type=F32) + ba_ref[...])
    gs = _sigmoid(jnp.dot(hn, wgs_ref[...], preferred_element_type=F32) + bs_ref[...])
    ya = jnp.dot(a_ref[...], pa_ref[...], preferred_element_type=F32)
    ys = jnp.dot(s_ref[...], pg_ref[...], preferred_element_type=F32)
    o_ref[...] = (ga * ya + gs * ys).astype(o_ref.dtype)


def _merge(hn, attn, sgu, wga, wgs, ba, bs, pa, pg, tm=512, tn=1024):
    s, d = hn.shape
    return pl.pallas_call(
        _merge_kernel,
        grid=(d // tn, s // tm),
        in_specs=[pl.BlockSpec((tm, d), lambda j, i: (i, 0)),
                  pl.BlockSpec((tm, ATTN_WIDTH), lambda j, i: (i, 0)),
                  pl.BlockSpec((tm, SGU_WIDTH), lambda j, i: (i, 0)),
                  pl.BlockSpec((d, tn), lambda j, i: (0, j)),
                  pl.BlockSpec((d, tn), lambda j, i: (0, j)),
                  pl.BlockSpec((1, tn), lambda j, i: (0, j)),
                  pl.BlockSpec((1, tn), lambda j, i: (0, j)),
                  pl.BlockSpec((ATTN_WIDTH, tn), lambda j, i: (0, j)),
                  pl.BlockSpec((SGU_WIDTH, tn), lambda j, i: (0, j))],
        out_specs=pl.BlockSpec((tm, tn), lambda j, i: (i, j)),
        out_shape=jax.ShapeDtypeStruct((s, d), BF16),
        compiler_params=_params(("parallel", "parallel")),
        name="gated_merge",
    )(hn, attn, sgu, wga, wgs, ba, bs, pa, pg)


def _out_kernel(x_ref, mg_ref, wo_ref, g_ref, h_ref, xn_ref):
    h = x_ref[...] + jnp.dot(mg_ref[...], wo_ref[...], preferred_element_type=F32)
    h_ref[...] = h
    ms = jnp.mean(h * h, axis=-1, keepdims=True)
    xn_ref[...] = (h * lax.rsqrt(ms + EPS) * g_ref[...]).astype(xn_ref.dtype)


def _out_proj(x, merged, wo, g, tm=512):
    s, d = x.shape
    return pl.pallas_call(
        _out_kernel,
        grid=(s // tm,),
        in_specs=[pl.BlockSpec((tm, d), lambda i: (i, 0)),
                  pl.BlockSpec((tm, d), lambda i: (i, 0)),
                  pl.BlockSpec((d, d), lambda i: (0, 0)),
                  pl.BlockSpec((1, d), lambda i: (0, 0))],
        out_specs=[pl.BlockSpec((tm, d), lambda i: (i, 0)),
                   pl.BlockSpec((tm, d), lambda i: (i, 0))],
        out_shape=[jax.ShapeDtypeStruct((s, d), F32),
                   jax.ShapeDtypeStruct((s, d), BF16)],
        compiler_params=_params(("parallel",)),
        name="out_proj_norm",
    )(x, merged, wo, g)


def _sorting_network(n):
    def merge(lo, hi, r):
        step = r * 2
        if step < hi - lo:
            yield from merge(lo, hi, step)
            yield from merge(lo + r, hi, step)
            yield from [(i, i + r) for i in range(lo + r, hi - r, step)]
        else:
            yield (lo, lo + r)

    def sort(lo, hi):
        if hi - lo >= 1:
            mid = lo + (hi - lo) // 2
            yield from sort(lo, mid)
            yield from sort(mid + 1, hi)
            yield from merge(lo, hi, 1)

    return tuple(sort(0, n - 1))


def _compare_exchange(x, i, j):
    a, b = x[i], x[j]
    if b is None:
        return
    if a is None:
        x[i], x[j] = b, None
        return
    x[i], x[j] = jnp.maximum(a, b), jnp.minimum(a, b)


def _sort_desc(x):
    for i, j in _sorting_network(len(x)):
        _compare_exchange(x, i, j)
    return x


def _bitonic_sort_desc(x):
    n = len(x)
    stride = n // 2
    while stride >= 1:
        for i in range(n):
            if i & stride == 0:
                _compare_exchange(x, i, i + stride)
        stride //= 2
    return x


def _max_of(vals):
    vals = [v for v in vals if v is not None]
    while len(vals) > 1:
        vals = [jnp.maximum(vals[i], vals[i + 1]) for i in range(0, len(vals) - 1, 2)] + \
               ([vals[-1]] if len(vals) % 2 else [])
    return vals[0] if vals else None


def _merge_with_sublane_neighbour(x, runner_up, shift):
    n = len(x)

    def roll(a):
        return None if a is None else pltpu.roll(a, shift, axis=0)

    y = [roll(a) for a in x]
    hi, lo = [], []
    for i in range(n):
        a, b = x[i], y[n - 1 - i]
        if a is None or b is None:
            hi.append(b if a is None else a)
            lo.append(None)
        else:
            hi.append(jnp.maximum(a, b))
            lo.append(jnp.minimum(a, b))
    return _bitonic_sort_desc(hi), _max_of(lo + [runner_up, roll(runner_up)])


def _rank_over_sublanes_and_tiles(tiles):
    x, runner_up = _sort_desc(list(tiles)), None
    for shift in (4, 2, 1):
        x, runner_up = _merge_with_sublane_neighbour(x, runner_up, shift)
    return x + [runner_up]


def _peer_route_kernel(xn_ref, wq_ref, k1_ref, k2_ref,
                       a1_ref, p1_ref, s2_ref, p2_ref, st_scr):
    n_lt = st_scr.shape[1]
    pq = jnp.dot(xn_ref[...], wq_ref[...], preferred_element_type=F32).astype(BF16)
    nt = (((1,), (1,)), ((), ()))
    for h in range(PEER_HEADS):
        for half, k_ref in enumerate((k1_ref, k2_ref)):
            lo = h * PEER_QUERY_DIM + half * PEER_HALF
            sc = lax.dot_general(k_ref[h], pq[:, lo:lo + PEER_HALF], nt, preferred_element_type=F32)
            for lt in range(n_lt):
                st_scr[2 * h + half, lt] = sc[:, lt * LANES:(lt + 1) * LANES]

    n_tiles = PEER_N_KEYS // SUBLANES
    sub = lax.broadcasted_iota(jnp.int32, (SUBLANES, LANES), 0)

    def spread(vals):
        out = vals[-1]
        for r in range(len(vals) - 2, -1, -1):
            out = jnp.where(sub == r, vals[r], out)
        return out

    def per_slab(idx, carry):
        h = idx // n_lt
        lt = idx % n_lt
        s1 = st_scr[2 * h, lt]
        s2 = st_scr[2 * h + 1, lt]
        v1 = _rank_over_sublanes_and_tiles([s1[t * SUBLANES:(t + 1) * SUBLANES] for t in range(n_tiles)])
        v2 = _rank_over_sublanes_and_tiles([s2[t * SUBLANES:(t + 1) * SUBLANES] for t in range(n_tiles)])
        v2_lo, v2_hi = spread(v2[0:SUBLANES]), spread(v2[SUBLANES:2 * SUBLANES])
        v1_hi = spread(v1[SUBLANES:2 * SUBLANES])
        corner = jnp.where(sub == 0, v1[0] + v2[PEER_TOPK],
                           jnp.where(sub == 1, v1[PEER_TOPK] + v2[0], -jnp.inf))
        cands = [v1[0] + v2_lo, v1[0] + v2_hi] + [v1[i] + v2_lo for i in range(1, SUBLANES)]
        cands += [v1_hi + v2[0], corner]
        top = _rank_over_sublanes_and_tiles(cands + [None] * (n_tiles - len(cands)))
        cut = 0.5 * (top[PEER_TOPK - 1] + top[PEER_TOPK])
        z = jnp.ones((SUBLANES, LANES), F32)
        for r in range(1, PEER_TOPK):
            z = z + jnp.exp(top[r] - top[0])
        reps = (n_tiles, 1)
        a1_ref[h, lt] = jnp.tile(cut, reps) - s1
        s2_ref[h, lt] = s2
        p1_ref[h, lt] = jnp.exp(s1 - jnp.tile(v1[0], reps))
        p2_ref[h, lt] = jnp.exp(s2 - jnp.tile(v2[0], reps)) * jnp.tile(0.5 / z, reps)
        return carry

    lax.fori_loop(0, PEER_HEADS * n_lt, per_slab, 0)


def _peer_route(xn, wq, k1, k2, tm=512):
    s, d = xn.shape
    n_lt = tm // LANES
    stat = jax.ShapeDtypeStruct((PEER_HEADS, s // LANES, PEER_N_KEYS, LANES), F32)
    stat_spec = pl.BlockSpec((PEER_HEADS, n_lt, PEER_N_KEYS, LANES), lambda i: (0, i, 0, 0))
    return pl.pallas_call(
        _peer_route_kernel,
        grid=(s // tm,),
        in_specs=[pl.BlockSpec((tm, d), lambda i: (i, 0)),
                  pl.BlockSpec((d, PEER_HEADS * PEER_QUERY_DIM), lambda i: (0, 0)),
                  pl.BlockSpec((PEER_HEADS, PEER_N_KEYS, PEER_HALF), lambda i: (0, 0, 0)),
                  pl.BlockSpec((PEER_HEADS, PEER_N_KEYS, PEER_HALF), lambda i: (0, 0, 0))],
        out_specs=[stat_spec, stat_spec, stat_spec, stat_spec],
        out_shape=[stat, stat, stat, stat],
        scratch_shapes=[pltpu.VMEM((2 * PEER_HEADS, n_lt, PEER_N_KEYS, LANES), F32)],
        compiler_params=_params(("parallel",)),
        name="peer_route",
    )(xn, wq, k1, k2)


def _peer_dense_kernel(xn_ref, h_ref, a1_ref, p1_ref, s2_ref, p2_ref, u_ref, v_ref, o_ref, at_scr):
    eb = pl.program_id(1)
    tt = xn_ref.shape[0]
    n_e1 = a1_ref.shape[2]
    e1_per_chain = n_e1 // N_CHAINS
    ce = e1_per_chain * PEER_N_KEYS

    @pl.when(eb == 0)
    def _():
        o_ref[...] = h_ref[...]

    xn = xn_ref[...]

    def pre_activations(c):
        return lax.dot_general(xn, u_ref[c * ce:(c + 1) * ce, :], (((1,), (1,)), ((), ())),
                               preferred_element_type=F32)

    st_next = pre_activations(0)
    for c in range(N_CHAINS):
        es = slice(c * ce, (c + 1) * ce)
        st = st_next
        if c + 1 < N_CHAINS:
            st_next = pre_activations(c + 1)
        for jj in range(e1_per_chain):
            j = c * e1_per_chain + jj
            for lt in range(tt // LANES):
                cols = slice(lt * LANES, (lt + 1) * LANES)
                sb = st[lt * LANES:(lt + 1) * LANES, jj * PEER_N_KEYS:(jj + 1) * PEER_N_KEYS]
                w = jnp.zeros((PEER_N_KEYS, LANES), F32)
                for h in range(PEER_HEADS):
                    keep = s2_ref[h, lt] >= a1_ref[h, lt, j:j + 1, :]
                    w = w + p1_ref[h, lt, j:j + 1, :] * jnp.where(keep, p2_ref[h, lt], 0.0)
                at_scr[cols, j * PEER_N_KEYS:(j + 1) * PEER_N_KEYS] = _gelu_times_twice(sb, w.T).astype(at_scr.dtype)
        o_ref[...] += jnp.dot(at_scr[:, es], v_ref[es, :], preferred_element_type=F32)


def _peer_dense(xn, h1, a1, p1, s2, p2, u, v, tt=512, te=1024):
    s, d = xn.shape
    n_e1 = te // PEER_N_KEYS
    n_lt = tt // LANES
    return pl.pallas_call(
        _peer_dense_kernel,
        grid=(s // tt, PEER_N_EXPERTS // te),
        in_specs=[pl.BlockSpec((tt, d), lambda t, e: (t, 0)),
                  pl.BlockSpec((tt, d), lambda t, e: (t, 0)),
                  pl.BlockSpec((PEER_HEADS, n_lt, n_e1, LANES), lambda t, e: (0, t, e, 0)),
                  pl.BlockSpec((PEER_HEADS, n_lt, n_e1, LANES), lambda t, e: (0, t, e, 0)),
                  pl.BlockSpec((PEER_HEADS, n_lt, PEER_N_KEYS, LANES), lambda t, e: (0, t, 0, 0)),
                  pl.BlockSpec((PEER_HEADS, n_lt, PEER_N_KEYS, LANES), lambda t, e: (0, t, 0, 0)),
                  pl.BlockSpec((te, d), lambda t, e: (e, 0)),
                  pl.BlockSpec((te, d), lambda t, e: (e, 0))],
        out_specs=pl.BlockSpec((tt, d), lambda t, e: (t, 0)),
        out_shape=jax.ShapeDtypeStruct((s, d), F32),
        scratch_shapes=[pltpu.VMEM((tt, te), BF16)],
        compiler_params=_params(("parallel", "arbitrary")),
        name="peer_dense",
    )(xn, h1, a1, p1, s2, p2, u, v)


def _rope_tables(seq_len):
    rows = seq_len // GRID_W
    row = jnp.repeat(jnp.arange(rows, dtype=F32), GRID_W)
    col = jnp.tile(jnp.arange(GRID_W, dtype=F32), rows)
    n_pairs = HEAD_DIM // 4
    inv_freq = ROPE_THETA ** (-jnp.arange(n_pairs, dtype=F32) / n_pairs)
    ang = jnp.concatenate([row[:, None] * inv_freq, col[:, None] * inv_freq], axis=-1)
    cos, sin = jnp.cos(ang), jnp.sin(ang)
    return jnp.concatenate([cos, cos], axis=-1), jnp.concatenate([-sin, sin], axis=-1)


def kernel(x, norm_mix, w_in, b_gate, q_norm, k_norm, sgu_norm, w_sgu, b_sgu, w_attn_proj,
           w_sgu_proj, w_out, norm_ffn, w_peer_q, peer_k1, peer_k2, peer_u, peer_v):
    b, s, d = x.shape
    assert b == 1 and norm_mix.shape[0] == 1
    l = 0
    xs = x.reshape(s, d)

    pair_perm = np.concatenate([np.arange(0, HEAD_DIM, 2), np.arange(1, HEAD_DIM, 2)])
    q_cols = (np.arange(N_Q_HEADS)[:, None] * HEAD_DIM + pair_perm[None, :]).reshape(-1)
    k_cols = ATTN_WIDTH + (np.arange(N_KV_HEADS)[:, None] * HEAD_DIM + pair_perm[None, :]).reshape(-1)
    v_cols = ATTN_WIDTH + KV_WIDTH + np.arange(KV_WIDTH)
    w = w_in[l]
    o_su = ATTN_WIDTH + 2 * KV_WIDTH
    o_sv = o_su + SGU_WIDTH
    o_g = o_sv + SGU_WIDTH
    w_qkv = w[:, np.concatenate([q_cols, k_cols, v_cols])].astype(BF16)
    w_su = w[:, o_su:o_sv].astype(BF16)
    w_sv = w[:, o_sv:o_g].astype(BF16)
    w_ga = w[:, o_g:o_g + d].astype(BF16)
    w_gs = w[:, o_g + d:].astype(BF16)
    cos2, sin2 = _rope_tables(s)

    hn, q, k, v = _norm_qkv(xs, norm_mix[l].reshape(1, d), w_qkv, q_norm[l][pair_perm].reshape(1, HEAD_DIM),
                            k_norm[l][pair_perm].reshape(1, HEAD_DIM), cos2, sin2)
    attn = _attention(q, k, v)
    bias_s = jnp.broadcast_to(b_sgu[l][:, :, None], (SGU_GROUPS, SGU_CHUNK, 128))
    sgu = _sgu(hn, w_su, w_sv, sgu_norm[l].reshape(1, SGU_WIDTH), w_sgu[l].astype(BF16), bias_s)
    merged = _merge(hn, attn, sgu, w_ga, w_gs, b_gate[l][:d].reshape(1, d), b_gate[l][d:].reshape(1, d),
                    w_attn_proj[l].astype(BF16), w_sgu_proj[l].astype(BF16))
    h1, xn2 = _out_proj(xs, merged, w_out[l].astype(BF16), norm_ffn[l].reshape(1, d))
    a1, p1, s2, p2 = _peer_route(xn2, w_peer_q[l].astype(BF16), peer_k1[l].astype(BF16),
                                 peer_k2[l].astype(BF16))
    out = _peer_dense(xn2, h1, a1, p1, s2, p2, peer_u[l].astype(BF16), peer_v[l].astype(BF16))
    return out.reshape(b, s, d)
```

```python
import math

import jax
import jax.numpy as jnp
import numpy as np
from jax import lax
from jax.experimental import pallas as pl
from jax.experimental.pallas import tpu as pltpu

F32 = jnp.float32
BF16 = jnp.bfloat16

GRID_W = 64
ROPE_THETA = 10000.0
N_Q_HEADS = 8
N_KV_HEADS = 2
HEAD_DIM = 128
GQA_GROUP = N_Q_HEADS // N_KV_HEADS
ATTN_WIDTH = N_Q_HEADS * HEAD_DIM
KV_WIDTH = N_KV_HEADS * HEAD_DIM
SGU_GROUPS = 8
SGU_CHUNK = 128
SGU_WIDTH = SGU_GROUPS * 128
PEER_HEADS = 8
PEER_N_KEYS = 128
PEER_N_EXPERTS = PEER_N_KEYS * PEER_N_KEYS
PEER_QUERY_DIM = 256
PEER_HALF = PEER_QUERY_DIM // 2
PEER_TOPK = 16
EPS = 1e-6

LANES = 128
SUBLANES = 8
VMEM_LIMIT = 56 * 1024 * 1024

N_RANK = PEER_TOPK + 1
ATTN_KEY_CHUNK = 2048
N_CHAINS = 4


def _gelu(x):
    c = math.sqrt(2.0 / math.pi)
    half = 0.5 * x
    return half + half * jnp.tanh(x * (c + (c * 0.044715) * (x * x)))


def _gelu_times_twice(x, w_half):
    c = math.sqrt(2.0 / math.pi)
    xw = x * w_half
    return xw + xw * jnp.tanh(x * (c + (c * 0.044715) * (x * x)))


def _sigmoid(x):
    return 1.0 / (1.0 + jnp.exp(-x))


def _params(sem, vmem=VMEM_LIMIT):
    return pltpu.CompilerParams(dimension_semantics=sem, vmem_limit_bytes=vmem)


def _qkv_kernel(x_ref, g_ref, w_ref, qg_ref, kg_ref, c_ref, s_ref, hn_ref, q_ref, k_ref, v_ref):
    x = x_ref[...]
    ms = jnp.mean(x * x, axis=-1, keepdims=True)
    hn = (x * lax.rsqrt(ms + EPS) * g_ref[...]).astype(hn_ref.dtype)
    hn_ref[...] = hn
    z = jnp.dot(hn, w_ref[...], preferred_element_type=F32)
    c = c_ref[...]
    s = s_ref[...]

    def norm_rope(xh, g):
        ms = jnp.mean(xh * xh, axis=-1, keepdims=True)
        y = xh * lax.rsqrt(ms + EPS) * g
        return y * c + pltpu.roll(y, HEAD_DIM // 2, axis=1) * s

    scale = math.log2(math.e) / math.sqrt(HEAD_DIM)
    for h in range(N_Q_HEADS):
        sl = slice(h * HEAD_DIM, (h + 1) * HEAD_DIM)
        q_ref[:, sl] = (norm_rope(z[:, sl], qg_ref[...]) * scale).astype(q_ref.dtype)
    ones = jnp.ones((z.shape[0], HEAD_DIM), v_ref.dtype)
    for h in range(N_KV_HEADS):
        sl = slice(h * HEAD_DIM, (h + 1) * HEAD_DIM)
        zs = slice(ATTN_WIDTH + h * HEAD_DIM, ATTN_WIDTH + (h + 1) * HEAD_DIM)
        k_ref[:, sl] = norm_rope(z[:, zs], kg_ref[...]).astype(k_ref.dtype)
        vs = slice(ATTN_WIDTH + KV_WIDTH + h * HEAD_DIM, ATTN_WIDTH + KV_WIDTH + (h + 1) * HEAD_DIM)
        v_ref[:, 2 * h * HEAD_DIM:(2 * h + 1) * HEAD_DIM] = z[:, vs].astype(v_ref.dtype)
        v_ref[:, (2 * h + 1) * HEAD_DIM:(2 * h + 2) * HEAD_DIM] = ones


def _norm_qkv(x, g, w_qkv, qg, kg, cos2, sin2, tm=512):
    s, d = x.shape
    n = w_qkv.shape[1]
    return pl.pallas_call(
        _qkv_kernel,
        grid=(s // tm,),
        in_specs=[pl.BlockSpec((tm, d), lambda i: (i, 0)),
                  pl.BlockSpec((1, d), lambda i: (0, 0)),
                  pl.BlockSpec((d, n), lambda i: (0, 0)),
                  pl.BlockSpec((1, HEAD_DIM), lambda i: (0, 0)),
                  pl.BlockSpec((1, HEAD_DIM), lambda i: (0, 0)),
                  pl.BlockSpec((tm, HEAD_DIM), lambda i: (i, 0)),
                  pl.BlockSpec((tm, HEAD_DIM), lambda i: (i, 0))],
        out_specs=[pl.BlockSpec((tm, d), lambda i: (i, 0)),
                   pl.BlockSpec((tm, ATTN_WIDTH), lambda i: (i, 0)),
                   pl.BlockSpec((tm, KV_WIDTH), lambda i: (i, 0)),
                   pl.BlockSpec((tm, 2 * KV_WIDTH), lambda i: (i, 0))],
        out_shape=[jax.ShapeDtypeStruct((s, d), BF16),
                   jax.ShapeDtypeStruct((s, ATTN_WIDTH), BF16),
                   jax.ShapeDtypeStruct((s, KV_WIDTH), BF16),
                   jax.ShapeDtypeStruct((s, 2 * KV_WIDTH), BF16)],
        compiler_params=_params(("parallel",)),
        name="norm_qkv_rope",
    )(x, g, w_qkv, qg, kg, cos2, sin2)


def _key_chunks(tk):
    edge = ATTN_KEY_CHUNK // 4
    if tk < 2 * ATTN_KEY_CHUNK:
        return [min(tk, ATTN_KEY_CHUNK)] * max(1, tk // ATTN_KEY_CHUNK)
    n_full = tk // ATTN_KEY_CHUNK - 1
    return [edge] + [ATTN_KEY_CHUNK] * n_full + [ATTN_KEY_CHUNK - 2 * edge, edge]


def _attn_kernel(q_ref, k_ref, v_ref, o_ref, m_ref, acc_ref):
    ki = pl.program_id(2)
    tk = k_ref.shape[0]

    @pl.when(ki == 0)
    def _():
        m_ref[...] = jnp.full(m_ref.shape, -jnp.inf, F32)
        acc_ref[...] = jnp.zeros(acc_ref.shape, F32)

    lo = 0
    for kc in _key_chunks(tk):
        k = k_ref[lo:lo + kc, :]
        v = v_ref[lo:lo + kc, :]
        lo += kc
        for g in range(GQA_GROUP):
            q = q_ref[:, g * HEAD_DIM:(g + 1) * HEAD_DIM]
            s = lax.dot_general(q, k, (((1,), (1,)), ((), ())), preferred_element_type=F32)
            m_prev = m_ref[g]
            m_new = jnp.maximum(m_prev, jnp.max(s, axis=1, keepdims=True))
            alpha = jnp.exp2(m_prev - m_new)
            p = jnp.exp2(s - jnp.tile(m_new, (1, kc // LANES)))
            acc_ref[g] = jnp.tile(alpha, (1, 2)) * acc_ref[g] + jnp.dot(p.astype(v.dtype), v,
                                                                         preferred_element_type=F32)
            m_ref[g] = m_new

    @pl.when(ki == pl.num_programs(2) - 1)
    def _():
        for g in range(GQA_GROUP):
            acc = acc_ref[g]
            o_ref[:, g * HEAD_DIM:(g + 1) * HEAD_DIM] = (acc[:, :HEAD_DIM] / acc[:, HEAD_DIM:]).astype(o_ref.dtype)


def _attention(q, k, v, tq=512, tk=8192):
    s = q.shape[0]
    gw = GQA_GROUP * HEAD_DIM
    return pl.pallas_call(
        _attn_kernel,
        grid=(N_KV_HEADS, s // tq, s // tk),
        in_specs=[pl.BlockSpec((tq, gw), lambda h, i, j: (i, h)),
                  pl.BlockSpec((tk, HEAD_DIM), lambda h, i, j: (j, h)),
                  pl.BlockSpec((tk, 2 * HEAD_DIM), lambda h, i, j: (j, h))],
        out_specs=pl.BlockSpec((tq, gw), lambda h, i, j: (i, h)),
        out_shape=jax.ShapeDtypeStruct((s, ATTN_WIDTH), BF16),
        scratch_shapes=[pltpu.VMEM((GQA_GROUP, tq, LANES), F32),
                        pltpu.VMEM((GQA_GROUP, tq, 2 * HEAD_DIM), F32)],
        compiler_params=_params(("parallel", "parallel", "arbitrary")),
        name="flash_attention",
    )(q, k, v)


def _sgu_kernel(hn_ref, wu_ref, wv_ref, gn_ref, ws_ref, bs_ref, o_ref):
    hn = hn_ref[...]
    tm = hn.shape[0]
    u = _gelu(jnp.dot(hn, wu_ref[...], preferred_element_type=F32))
    v = _gelu(jnp.dot(hn, wv_ref[...], preferred_element_type=F32))
    ms = jnp.mean(v * v, axis=-1, keepdims=True)
    vn = (v * lax.rsqrt(ms + EPS) * gn_ref[...]).astype(BF16)
    for c in range(tm // SGU_CHUNK):
        rows = slice(c * SGU_CHUNK, (c + 1) * SGU_CHUNK)
        for g in range(SGU_GROUPS):
            cols = slice(g * 128, (g + 1) * 128)
            mixed = jnp.dot(ws_ref[g], vn[rows, cols], preferred_element_type=F32) + bs_ref[g]
            o_ref[rows, cols] = (u[rows, cols] * mixed).astype(o_ref.dtype)


def _sgu(hn, wu, wv, gn, ws, bs, tm=512):
    s, d = hn.shape
    return pl.pallas_call(
        _sgu_kernel,
        grid=(s // tm,),
        in_specs=[pl.BlockSpec((tm, d), lambda i: (i, 0)),
                  pl.BlockSpec((d, SGU_WIDTH), lambda i: (0, 0)),
                  pl.BlockSpec((d, SGU_WIDTH), lambda i: (0, 0)),
                  pl.BlockSpec((1, SGU_WIDTH), lambda i: (0, 0)),
                  pl.BlockSpec((SGU_GROUPS, SGU_CHUNK, SGU_CHUNK), lambda i: (0, 0, 0)),
                  pl.BlockSpec((SGU_GROUPS, SGU_CHUNK, 128), lambda i: (0, 0, 0))],
        out_specs=pl.BlockSpec((tm, SGU_WIDTH), lambda i: (i, 0)),
        out_shape=jax.ShapeDtypeStruct((s, SGU_WIDTH), BF16),
        compiler_params=_params(("parallel",)),
        name="sgu",
    )(hn, wu, wv, gn, ws, bs)


def _merge_kernel(hn_ref, a_ref, s_ref, wga_ref, wgs_ref, ba_ref, bs_ref, pa_ref, pg_ref, o_ref):
    hn = hn_ref[...]
    ga = _sigmoid(jnp.dot(hn, wga_ref[...], preferred_element_type=F32) + ba_ref[...])
    gs = _sigmoid(jnp.dot(hn, wgs_ref[...], preferred_element_type=F32) + bs_ref[...])
    ya = jnp.dot(a_ref[...], pa_ref[...], preferred_element_type=F32)
    ys = jnp.dot(s_ref[...], pg_ref[...], preferred_element_type=F32)
    o_ref[...] = (ga * ya + gs * ys).astype(o_ref.dtype)


def _merge(hn, attn, sgu, wga, wgs, ba, bs, pa, pg, tm=512, tn=1024):
    s, d = hn.shape
    return pl.pallas_call(
        _merge_kernel,
        grid=(d // tn, s // tm),
        in_specs=[pl.BlockSpec((tm, d), lambda j, i: (i, 0)),
                  pl.BlockSpec((tm, ATTN_WIDTH), lambda j, i: (i, 0)),
                  pl.BlockSpec((tm, SGU_WIDTH), lambda j, i: (i, 0)),
                  pl.BlockSpec((d, tn), lambda j, i: (0, j)),
                  pl.BlockSpec((d, tn), lambda j, i: (0, j)),
                  pl.BlockSpec((1, tn), lambda j, i: (0, j)),
                  pl.BlockSpec((1, tn), lambda j, i: (0, j)),
                  pl.BlockSpec((ATTN_WIDTH, tn), lambda j, i: (0, j)),
                  pl.BlockSpec((SGU_WIDTH, tn), lambda j, i: (0, j))],
        out_specs=pl.BlockSpec((tm, tn), lambda j, i: (i, j)),
        out_shape=jax.ShapeDtypeStruct((s, d), BF16),
        compiler_params=_params(("parallel", "parallel")),
        name="gated_merge",
    )(hn, attn, sgu, wga, wgs, ba, bs, pa, pg)


def _out_kernel(x_ref, mg_ref, wo_ref, g_ref, h_ref, xn_ref):
    h = x_ref[...] + jnp.dot(mg_ref[...], wo_ref[...], preferred_element_type=F32)
    h_ref[...] = h
    ms = jnp.mean(h * h, axis=-1, keepdims=True)
    xn_ref[...] = (h * lax.rsqrt(ms + EPS) * g_ref[...]).astype(xn_ref.dtype)


def _out_proj(x, merged, wo, g, tm=512):
    s, d = x.shape
    return pl.pallas_call(
        _out_kernel,
        grid=(s // tm,),
        in_specs=[pl.BlockSpec((tm, d), lambda i: (i, 0)),
                  pl.BlockSpec((tm, d), lambda i: (i, 0)),
                  pl.BlockSpec((d, d), lambda i: (0, 0)),
                  pl.BlockSpec((1, d), lambda i: (0, 0))],
        out_specs=[pl.BlockSpec((tm, d), lambda i: (i, 0)),
                   pl.BlockSpec((tm, d), lambda i: (i, 0))],
        out_shape=[jax.ShapeDtypeStruct((s, d), F32),
                   jax.ShapeDtypeStruct((s, d), BF16)],
        compiler_params=_params(("parallel",)),
        name="out_proj_norm",
    )(x, merged, wo, g)


def _sorting_network(n):
    def merge(lo, hi, r):
        step = r * 2
        if step < hi - lo:
            yield from merge(lo, hi, step)
            yield from merge(lo + r, hi, step)
            yield from [(i, i + r) for i in range(lo + r, hi - r, step)]
        else:
            yield (lo, lo + r)

    def sort(lo, hi):
        if hi - lo >= 1:
            mid = lo + (hi - lo) // 2
            yield from sort(lo, mid)
            yield from sort(mid + 1, hi)
            yield from merge(lo, hi, 1)

    return tuple(sort(0, n - 1))


def _compare_exchange(x, i, j):
    a, b = x[i], x[j]
    if b is None:
        return
    if a is None:
        x[i], x[j] = b, None
        return
    x[i], x[j] = jnp.maximum(a, b), jnp.minimum(a, b)


def _sort_desc(x):
    for i, j in _sorting_network(len(x)):
        _compare_exchange(x, i, j)
    return x


def _bitonic_sort_desc(x):
    n = len(x)
    stride = n // 2
    while stride >= 1:
        for i in range(n):
            if i & stride == 0:
                _compare_exchange(x, i, i + stride)
        stride //= 2
    return x


def _max_of(vals):
    vals = [v for v in vals if v is not None]
    while len(vals) > 1:
        vals = [jnp.maximum(vals[i], vals[i + 1]) for i in range(0, len(vals) - 1, 2)] + \
               ([vals[-1]] if len(vals) % 2 else [])
    return vals[0] if vals else None


def _merge_with_sublane_neighbour(x, runner_up, shift):
    n = len(x)

    def roll(a):
        return None if a is None else pltpu.roll(a, shift, axis=0)

    y = [roll(a) for a in x]
    hi, lo = [], []
    for i in range(n):
        a, b = x[i], y[n - 1 - i]
        if a is None or b is None:
            hi.append(b if a is None else a)
            lo.append(None)
        else:
            hi.append(jnp.maximum(a, b))
            lo.append(jnp.minimum(a, b))
    return _bitonic_sort_desc(hi), _max_of(lo + [runner_up, roll(runner_up)])


def _rank_over_sublanes_and_tiles(tiles):
    x, runner_up = _sort_desc(list(tiles)), None
    for shift in (4, 2, 1):
        x, runner_up = _merge_with_sublane_neighbour(x, runner_up, shift)
    return x + [runner_up]


def _peer_route_kernel(xn_ref, wq_ref, k1_ref, k2_ref,
                       a1_ref, p1_ref, s2_ref, p2_ref, st_scr):
    n_lt = st_scr.shape[1]
    pq = jnp.dot(xn_ref[...], wq_ref[...], preferred_element_type=F32).astype(BF16)
    nt = (((1,), (1,)), ((), ()))
    for h in range(PEER_HEADS):
        for half, k_ref in enumerate((k1_ref, k2_ref)):
            lo = h * PEER_QUERY_DIM + half * PEER_HALF
            sc = lax.dot_general(k_ref[h], pq[:, lo:lo + PEER_HALF], nt, preferred_element_type=F32)
            for lt in range(n_lt):
                st_scr[2 * h + half, lt] = sc[:, lt * LANES:(lt + 1) * LANES]

    n_tiles = PEER_N_KEYS // SUBLANES
    sub = lax.broadcasted_iota(jnp.int32, (SUBLANES, LANES), 0)

    def spread(vals):
        out = vals[-1]
        for r in range(len(vals) - 2, -1, -1):
            out = jnp.where(sub == r, vals[r], out)
        return out

    def per_slab(idx, carry):
        h = idx // n_lt
        lt = idx % n_lt
        s1 = st_scr[2 * h, lt]
        s2 = st_scr[2 * h + 1, lt]
        v1 = _rank_over_sublanes_and_tiles([s1[t * SUBLANES:(t + 1) * SUBLANES] for t in range(n_tiles)])
        v2 = _rank_over_sublanes_and_tiles([s2[t * SUBLANES:(t + 1) * SUBLANES] for t in range(n_tiles)])
        v2_lo, v2_hi = spread(v2[0:SUBLANES]), spread(v2[SUBLANES:2 * SUBLANES])
        v1_hi = spread(v1[SUBLANES:2 * SUBLANES])
        corner = jnp.where(sub == 0, v1[0] + v2[PEER_TOPK],
                           jnp.where(sub == 1, v1[PEER_TOPK] + v2[0], -jnp.inf))
        cands = [v1[0] + v2_lo, v1[0] + v2_hi] + [v1[i] + v2_lo for i in range(1, SUBLANES)]
        cands += [v1_hi + v2[0], corner]
        top = _rank_over_sublanes_and_tiles(cands + [None] * (n_tiles - len(cands)))
        cut = 0.5 * (top[PEER_TOPK - 1] + top[PEER_TOPK])
        z = jnp.ones((SUBLANES, LANES), F32)
        for r in range(1, PEER_TOPK):
            z = z + jnp.exp(top[r] - top[0])
        reps = (n_tiles, 1)
        a1_ref[h, lt] = jnp.tile(cut, reps) - s1
        s2_ref[h, lt] = s2
        p1_ref[h, lt] = jnp.exp(s1 - jnp.tile(v1[0], reps))
        p2_ref[h, lt] = jnp.exp(s2 - jnp.tile(v2[0], reps)) * jnp.tile(0.5 / z, reps)
        return carry

    lax.fori_loop(0, PEER_HEADS * n_lt, per_slab, 0)


def _peer_route(xn, wq, k1, k2, tm=512):
    s, d = xn.shape
    n_lt = tm // LANES
    stat = jax.ShapeDtypeStruct((PEER_HEADS, s // LANES, PEER_N_KEYS, LANES), F32)
    stat_spec = pl.BlockSpec((PEER_HEADS, n_lt, PEER_N_KEYS, LANES), lambda i: (0, i, 0, 0))
    return pl.pallas_call(
        _peer_route_kernel,
        grid=(s // tm,),
        in_specs=[pl.BlockSpec((tm, d), lambda i: (i, 0)),
                  pl.BlockSpec((d, PEER_HEADS * PEER_QUERY_DIM), lambda i: (0, 0)),
                  pl.BlockSpec((PEER_HEADS, PEER_N_KEYS, PEER_HALF), lambda i: (0, 0, 0)),
                  pl.BlockSpec((PEER_HEADS, PEER_N_KEYS, PEER_HALF), lambda i: (0, 0, 0))],
        out_specs=[stat_spec, stat_spec, stat_spec, stat_spec],
        out_shape=[stat, stat, stat, stat],
        scratch_shapes=[pltpu.VMEM((2 * PEER_HEADS, n_lt, PEER_N_KEYS, LANES), F32)],
        compiler_params=_params(("parallel",)),
        name="peer_route",
    )(xn, wq, k1, k2)


def _peer_dense_kernel(xn_ref, h_ref, a1_ref, p1_ref, s2_ref, p2_ref, u_ref, v_ref, o_ref):
    eb = pl.program_id(1)
    tt = xn_ref.shape[0]
    n_e1 = a1_ref.shape[2]
    e1_per_chain = n_e1 // N_CHAINS
    ce = e1_per_chain * PEER_N_KEYS

    @pl.when(eb == 0)
    def _():
        o_ref[...] = h_ref[...]

    xn = xn_ref[...]

    def pre_activations(c):
        return lax.dot_general(xn, u_ref[c * ce:(c + 1) * ce, :], (((1,), (1,)), ((), ())),
                               preferred_element_type=F32)

    st_next = pre_activations(0)
    for c in range(N_CHAINS):
        es = slice(c * ce, (c + 1) * ce)
        st = st_next
        if c + 1 < N_CHAINS:
            st_next = pre_activations(c + 1)
        row_blocks = []
        for lt in range(tt // LANES):
            col_blocks = []
            for jj in range(e1_per_chain):
                j = c * e1_per_chain + jj
                sb = st[lt * LANES:(lt + 1) * LANES, jj * PEER_N_KEYS:(jj + 1) * PEER_N_KEYS]
                w = jnp.zeros((PEER_N_KEYS, LANES), F32)
                for h in range(PEER_HEADS):
                    keep = s2_ref[h, lt] >= a1_ref[h, lt, j:j + 1, :]
                    w = w + p1_ref[h, lt, j:j + 1, :] * jnp.where(keep, p2_ref[h, lt], 0.0)
                col_blocks.append(_gelu_times_twice(sb, w.T))
            row_blocks.append(jnp.concatenate(col_blocks, axis=1))
        at = jnp.concatenate(row_blocks, axis=0).astype(BF16)
        o_ref[...] += jnp.dot(at, v_ref[es, :], preferred_element_type=F32)


def _peer_dense(xn, h1, a1, p1, s2, p2, u, v, tt=512, te=1024):
    s, d = xn.shape
    n_e1 = te // PEER_N_KEYS
    n_lt = tt // LANES
    return pl.pallas_call(
        _peer_dense_kernel,
        grid=(s // tt, PEER_N_EXPERTS // te),
        in_specs=[pl.BlockSpec((tt, d), lambda t, e: (t, 0)),
                  pl.BlockSpec((tt, d), lambda t, e: (t, 0)),
                  pl.BlockSpec((PEER_HEADS, n_lt, n_e1, LANES), lambda t, e: (0, t, e, 0)),
                  pl.BlockSpec((PEER_HEADS, n_lt, n_e1, LANES), lambda t, e: (0, t, e, 0)),
                  pl.BlockSpec((PEER_HEADS, n_lt, PEER_N_KEYS, LANES), lambda t, e: (0, t, 0, 0)),
                  pl.BlockSpec((PEER_HEADS, n_lt, PEER_N_KEYS, LANES), lambda t, e: (0, t, 0, 0)),
                  pl.BlockSpec((te, d), lambda t, e: (e, 0)),
                  pl.BlockSpec((te, d), lambda t, e: (e, 0))],
        out_specs=pl.BlockSpec((tt, d), lambda t, e: (t, 0)),
        out_shape=jax.ShapeDtypeStruct((s, d), F32),
        compiler_params=_params(("parallel", "arbitrary")),
        name="peer_dense",
    )(xn, h1, a1, p1, s2, p2, u, v)


def _rope_tables(seq_len):
    rows = seq_len // GRID_W
    row = jnp.repeat(jnp.arange(rows, dtype=F32), GRID_W)
    col = jnp.tile(jnp.arange(GRID_W, dtype=F32), rows)
    n_pairs = HEAD_DIM // 4
    inv_freq = ROPE_THETA ** (-jnp.arange(n_pairs, dtype=F32) / n_pairs)
    ang = jnp.concatenate([row[:, None] * inv_freq, col[:, None] * inv_freq], axis=-1)
    cos, sin = jnp.cos(ang), jnp.sin(ang)
    return jnp.concatenate([cos, cos], axis=-1), jnp.concatenate([-sin, sin], axis=-1)


def kernel(x, norm_mix, w_in, b_gate, q_norm, k_norm, sgu_norm, w_sgu, b_sgu, w_attn_proj,
           w_sgu_proj, w_out, norm_ffn, w_peer_q, peer_k1, peer_k2, peer_u, peer_v):
    b, s, d = x.shape
    assert b == 1 and norm_mix.shape[0] == 1
    l = 0
    xs = x.reshape(s, d)

    pair_perm = np.concatenate([np.arange(0, HEAD_DIM, 2), np.arange(1, HEAD_DIM, 2)])
    q_cols = (np.arange(N_Q_HEADS)[:, None] * HEAD_DIM + pair_perm[None, :]).reshape(-1)
    k_cols = ATTN_WIDTH + (np.arange(N_KV_HEADS)[:, None] * HEAD_DIM + pair_perm[None, :]).reshape(-1)
    v_cols = ATTN_WIDTH + KV_WIDTH + np.arange(KV_WIDTH)
    w = w_in[l]
    o_su = ATTN_WIDTH + 2 * KV_WIDTH
    o_sv = o_su + SGU_WIDTH
    o_g = o_sv + SGU_WIDTH
    w_qkv = w[:, np.concatenate([q_cols, k_cols, v_cols])].astype(BF16)
    w_su = w[:, o_su:o_sv].astype(BF16)
    w_sv = w[:, o_sv:o_g].astype(BF16)
    w_ga = w[:, o_g:o_g + d].astype(BF16)
    w_gs = w[:, o_g + d:].astype(BF16)
    cos2, sin2 = _rope_tables(s)

    hn, q, k, v = _norm_qkv(xs, norm_mix[l].reshape(1, d), w_qkv, q_norm[l][pair_perm].reshape(1, HEAD_DIM),
                            k_norm[l][pair_perm].reshape(1, HEAD_DIM), cos2, sin2)
    attn = _attention(q, k, v)
    bias_s = jnp.broadcast_to(b_sgu[l][:, :, None], (SGU_GROUPS, SGU_CHUNK, 128))
    sgu = _sgu(hn, w_su, w_sv, sgu_norm[l].reshape(1, SGU_WIDTH), w_sgu[l].astype(BF16), bias_s)
    merged = _merge(hn, attn, sgu, w_ga, w_gs, b_gate[l][:d].reshape(1, d), b_gate[l][d:].reshape(1, d),
                    w_attn_proj[l].astype(BF16), w_sgu_proj[l].astype(BF16))
    h1, xn2 = _out_proj(xs, merged, w_out[l].astype(BF16), norm_ffn[l].reshape(1, d))
    a1, p1, s2, p2 = _peer_route(xn2, w_peer_q[l].astype(BF16), peer_k1[l].astype(BF16),
                                 peer_k2[l].astype(BF16))
    out = _peer_dense(xn2, h1, a1, p1, s2, p2, peer_u[l].astype(BF16), peer_v[l].astype(BF16))
    return out.reshape(b, s, d)
```

```python
import math

import jax
import jax.numpy as jnp
import numpy as np
from jax import lax
from jax.experimental import pallas as pl
from jax.experimental.pallas import tpu as pltpu

F32 = jnp.float32
BF16 = jnp.bfloat16

GRID_W = 64
ROPE_THETA = 10000.0
N_Q_HEADS = 8
N_KV_HEADS = 2
HEAD_DIM = 128
GQA_GROUP = N_Q_HEADS // N_KV_HEADS
ATTN_WIDTH = N_Q_HEADS * HEAD_DIM
KV_WIDTH = N_KV_HEADS * HEAD_DIM
SGU_GROUPS = 8
SGU_CHUNK = 128
SGU_WIDTH = SGU_GROUPS * 128
PEER_HEADS = 8
PEER_N_KEYS = 128
PEER_N_EXPERTS = PEER_N_KEYS * PEER_N_KEYS
PEER_QUERY_DIM = 256
PEER_HALF = PEER_QUERY_DIM // 2
PEER_TOPK = 16
EPS = 1e-6

LANES = 128
SUBLANES = 8
VMEM_LIMIT = 56 * 1024 * 1024

N_RANK = PEER_TOPK + 1
ATTN_KEY_CHUNK = 2048
N_CHAINS = 4


def _gelu(x):
    c = math.sqrt(2.0 / math.pi)
    half = 0.5 * x
    return half + half * jnp.tanh(x * (c + (c * 0.044715) * (x * x)))


def _gelu_times_twice(x, w_half):
    c = math.sqrt(2.0 / math.pi)
    xw = x * w_half
    return xw + xw * jnp.tanh(x * (c + (c * 0.044715) * (x * x)))


def _sigmoid(x):
    return 1.0 / (1.0 + jnp.exp(-x))


def _params(sem, vmem=VMEM_LIMIT):
    return pltpu.CompilerParams(dimension_semantics=sem, vmem_limit_bytes=vmem)


def _qkv_kernel(x_ref, g_ref, w_ref, qg_ref, kg_ref, c_ref, s_ref, hn_ref, q_ref, k_ref, v_ref):
    x = x_ref[...]
    ms = jnp.mean(x * x, axis=-1, keepdims=True)
    hn = (x * lax.rsqrt(ms + EPS) * g_ref[...]).astype(hn_ref.dtype)
    hn_ref[...] = hn
    z = jnp.dot(hn, w_ref[...], preferred_element_type=F32)
    c = c_ref[...]
    s = s_ref[...]

    def norm_rope(xh, g):
        ms = jnp.mean(xh * xh, axis=-1, keepdims=True)
        y = xh * lax.rsqrt(ms + EPS) * g
        return y * c + pltpu.roll(y, HEAD_DIM // 2, axis=1) * s

    scale = math.log2(math.e) / math.sqrt(HEAD_DIM)
    for h in range(N_Q_HEADS):
        sl = slice(h * HEAD_DIM, (h + 1) * HEAD_DIM)
        q_ref[:, sl] = (norm_rope(z[:, sl], qg_ref[...]) * scale).astype(q_ref.dtype)
    ones = jnp.ones((z.shape[0], HEAD_DIM), v_ref.dtype)
    for h in range(N_KV_HEADS):
        sl = slice(h * HEAD_DIM, (h + 1) * HEAD_DIM)
        zs = slice(ATTN_WIDTH + h * HEAD_DIM, ATTN_WIDTH + (h + 1) * HEAD_DIM)
        k_ref[:, sl] = norm_rope(z[:, zs], kg_ref[...]).astype(k_ref.dtype)
        vs = slice(ATTN_WIDTH + KV_WIDTH + h * HEAD_DIM, ATTN_WIDTH + KV_WIDTH + (h + 1) * HEAD_DIM)
        v_ref[:, 2 * h * HEAD_DIM:(2 * h + 1) * HEAD_DIM] = z[:, vs].astype(v_ref.dtype)
        v_ref[:, (2 * h + 1) * HEAD_DIM:(2 * h + 2) * HEAD_DIM] = ones


def _norm_qkv(x, g, w_qkv, qg, kg, cos2, sin2, tm=512):
    s, d = x.shape
    n = w_qkv.shape[1]
    return pl.pallas_call(
        _qkv_kernel,
        grid=(s // tm,),
        in_specs=[pl.BlockSpec((tm, d), lambda i: (i, 0)),
                  pl.BlockSpec((1, d), lambda i: (0, 0)),
                  pl.BlockSpec((d, n), lambda i: (0, 0)),
                  pl.BlockSpec((1, HEAD_DIM), lambda i: (0, 0)),
                  pl.BlockSpec((1, HEAD_DIM), lambda i: (0, 0)),
                  pl.BlockSpec((tm, HEAD_DIM), lambda i: (i, 0)),
                  pl.BlockSpec((tm, HEAD_DIM), lambda i: (i, 0))],
        out_specs=[pl.BlockSpec((tm, d), lambda i: (i, 0)),
                   pl.BlockSpec((tm, ATTN_WIDTH), lambda i: (i, 0)),
                   pl.BlockSpec((tm, KV_WIDTH), lambda i: (i, 0)),
                   pl.BlockSpec((tm, 2 * KV_WIDTH), lambda i: (i, 0))],
        out_shape=[jax.ShapeDtypeStruct((s, d), BF16),
                   jax.ShapeDtypeStruct((s, ATTN_WIDTH), BF16),
                   jax.ShapeDtypeStruct((s, KV_WIDTH), BF16),
                   jax.ShapeDtypeStruct((s, 2 * KV_WIDTH), BF16)],
        compiler_params=_params(("parallel",)),
        name="norm_qkv_rope",
    )(x, g, w_qkv, qg, kg, cos2, sin2)


def _key_chunks(tk):
    edge = ATTN_KEY_CHUNK // 4
    if tk < 2 * ATTN_KEY_CHUNK:
        return [min(tk, ATTN_KEY_CHUNK)] * max(1, tk // ATTN_KEY_CHUNK)
    n_full = tk // ATTN_KEY_CHUNK - 1
    return [edge] + [ATTN_KEY_CHUNK] * n_full + [ATTN_KEY_CHUNK - 2 * edge, edge]


def _attn_kernel(q_ref, k_ref, v_ref, o_ref, m_ref, acc_ref):
    ki = pl.program_id(2)
    tk = k_ref.shape[0]

    @pl.when(ki == 0)
    def _():
        m_ref[...] = jnp.full(m_ref.shape, -jnp.inf, F32)
        acc_ref[...] = jnp.zeros(acc_ref.shape, F32)

    lo = 0
    for kc in _key_chunks(tk):
        k = k_ref[lo:lo + kc, :]
        v = v_ref[lo:lo + kc, :]
        lo += kc
        for g in range(GQA_GROUP):
            q = q_ref[:, g * HEAD_DIM:(g + 1) * HEAD_DIM]
            s = lax.dot_general(q, k, (((1,), (1,)), ((), ())), preferred_element_type=F32)
            m_prev = m_ref[g]
            m_new = jnp.maximum(m_prev, jnp.max(s, axis=1, keepdims=True))
            alpha = jnp.exp2(m_prev - m_new)
            p = jnp.exp2(s - jnp.tile(m_new, (1, kc // LANES)))
            acc_ref[g] = jnp.tile(alpha, (1, 2)) * acc_ref[g] + jnp.dot(p.astype(v.dtype), v,
                                                                         preferred_element_type=F32)
            m_ref[g] = m_new

    @pl.when(ki == pl.num_programs(2) - 1)
    def _():
        for g in range(GQA_GROUP):
            acc = acc_ref[g]
            o_ref[:, g * HEAD_DIM:(g + 1) * HEAD_DIM] = (acc[:, :HEAD_DIM] / acc[:, HEAD_DIM:]).astype(o_ref.dtype)


def _attention(q, k, v, tq=512, tk=8192):
    s = q.shape[0]
    gw = GQA_GROUP * HEAD_DIM
    return pl.pallas_call(
        _attn_kernel,
        grid=(N_KV_HEADS, s // tq, s // tk),
        in_specs=[pl.BlockSpec((tq, gw), lambda h, i, j: (i, h)),
                  pl.BlockSpec((tk, HEAD_DIM), lambda h, i, j: (j, h)),
                  pl.BlockSpec((tk, 2 * HEAD_DIM), lambda h, i, j: (j, h))],
        out_specs=pl.BlockSpec((tq, gw), lambda h, i, j: (i, h)),
        out_shape=jax.ShapeDtypeStruct((s, ATTN_WIDTH), BF16),
        scratch_shapes=[pltpu.VMEM((GQA_GROUP, tq, LANES), F32),
                        pltpu.VMEM((GQA_GROUP, tq, 2 * HEAD_DIM), F32)],
        compiler_params=_params(("parallel", "parallel", "arbitrary")),
        name="flash_attention",
    )(q, k, v)


def _sgu_kernel(hn_ref, wu_ref, wv_ref, gn_ref, ws_ref, bs_ref, o_ref):
    hn = hn_ref[...]
    tm = hn.shape[0]
    u = _gelu(jnp.dot(hn, wu_ref[...], preferred_element_type=F32))
    v = _gelu(jnp.dot(hn, wv_ref[...], preferred_element_type=F32))
    ms = jnp.mean(v * v, axis=-1, keepdims=True)
    vn = (v * lax.rsqrt(ms + EPS) * gn_ref[...]).astype(BF16)
    for c in range(tm // SGU_CHUNK):
        rows = slice(c * SGU_CHUNK, (c + 1) * SGU_CHUNK)
        for g in range(SGU_GROUPS):
            cols = slice(g * 128, (g + 1) * 128)
            mixed = jnp.dot(ws_ref[g], vn[rows, cols], preferred_element_type=F32) + bs_ref[g]
            o_ref[rows, cols] = (u[rows, cols] * mixed).astype(o_ref.dtype)


def _sgu(hn, wu, wv, gn, ws, bs, tm=512):
    s, d = hn.shape
    return pl.pallas_call(
        _sgu_kernel,
        grid=(s // tm,),
        in_specs=[pl.BlockSpec((tm, d), lambda i: (i, 0)),
                  pl.BlockSpec((d, SGU_WIDTH), lambda i: (0, 0)),
                  pl.BlockSpec((d, SGU_WIDTH), lambda i: (0, 0)),
                  pl.BlockSpec((1, SGU_WIDTH), lambda i: (0, 0)),
                  pl.BlockSpec((SGU_GROUPS, SGU_CHUNK, SGU_CHUNK), lambda i: (0, 0, 0)),
                  pl.BlockSpec((SGU_GROUPS, SGU_CHUNK, 128), lambda i: (0, 0, 0))],
        out_specs=pl.BlockSpec((tm, SGU_WIDTH), lambda i: (i, 0)),
        out_shape=jax.ShapeDtypeStruct((s, SGU_WIDTH), BF16),
        compiler_params=_params(("parallel",)),
        name="sgu",
    )(hn, wu, wv, gn, ws, bs)


def _merge_kernel(hn_ref, a_ref, s_ref, wga_ref, wgs_ref, ba_ref, bs_ref, pa_ref, pg_ref, o_ref):
    hn = hn_ref[...]
    ga = _sigmoid(jnp.dot(hn, wga_ref[...], preferred_element_type=F32) + ba_ref[...])
    gs = _sigmoid(jnp.dot(hn, wgs_ref[...], preferred_element_type=F32) + bs_ref[...])
    ya = jnp.dot(a_ref[...], pa_ref[...], preferred_element_type=F32)
    ys = jnp.dot(s_ref[...], pg_ref[...], preferred_element_type=F32)
    o_ref[...] = (ga * ya + gs * ys).astype(o_ref.dtype)


def _merge(hn, attn, sgu, wga, wgs, ba, bs, pa, pg, tm=512, tn=1024):
    s, d = hn.shape
    return pl.pallas_call(
        _merge_kernel,
        grid=(d // tn, s // tm),
        in_specs=[pl.BlockSpec((tm, d), lambda j, i: (i, 0)),
                  pl.BlockSpec((tm, ATTN_WIDTH), lambda j, i: (i, 0)),
                  pl.BlockSpec((tm, SGU_WIDTH), lambda j, i: (i, 0)),
                  pl.BlockSpec((d, tn), lambda j, i: (0, j)),
                  pl.BlockSpec((d, tn), lambda j, i: (0, j)),
                  pl.BlockSpec((1, tn), lambda j, i: (0, j)),
                  pl.BlockSpec((1, tn), lambda j, i: (0, j)),
                  pl.BlockSpec((ATTN_WIDTH, tn), lambda j, i: (0, j)),
                  pl.BlockSpec((SGU_WIDTH, tn), lambda j, i: (0, j))],
        out_specs=pl.BlockSpec((tm, tn), lambda j, i: (i, j)),
        out_shape=jax.ShapeDtypeStruct((s, d), BF16),
        compiler_params=_params(("parallel", "parallel")),
        name="gated_merge",
    )(hn, attn, sgu, wga, wgs, ba, bs, pa, pg)


def _out_kernel(x_ref, mg_ref, wo_ref, g_ref, h_ref, xn_ref):
    h = x_ref[...] + jnp.dot(mg_ref[...], wo_ref[...], preferred_element_type=F32)
    h_ref[...] = h
    ms = jnp.mean(h * h, axis=-1, keepdims=True)
    xn_ref[...] = (h * lax.rsqrt(ms + EPS) * g_ref[...]).astype(xn_ref.dtype)


def _out_proj(x, merged, wo, g, tm=512):
    s, d = x.shape
    return pl.pallas_call(
        _out_kernel,
        grid=(s // tm,),
        in_specs=[pl.BlockSpec((tm, d), lambda i: (i, 0)),
                  pl.BlockSpec((tm, d), lambda i: (i, 0)),
                  pl.BlockSpec((d, d), lambda i: (0, 0)),
                  pl.BlockSpec((1, d), lambda i: (0, 0))],
        out_specs=[pl.BlockSpec((tm, d), lambda i: (i, 0)),
                   pl.BlockSpec((tm, d), lambda i: (i, 0))],
        out_shape=[jax.ShapeDtypeStruct((s, d), F32),
                   jax.ShapeDtypeStruct((s, d), BF16)],
        compiler_params=_params(("parallel",)),
        name="out_proj_norm",
    )(x, merged, wo, g)


def _sorting_network(n):
    def merge(lo, hi, r):
        step = r * 2
        if step < hi - lo:
            yield from merge(lo, hi, step)
            yield from merge(lo + r, hi, step)
            yield from [(i, i + r) for i in range(lo + r, hi - r, step)]
        else:
            yield (lo, lo + r)

    def sort(lo, hi):
        if hi - lo >= 1:
            mid = lo + (hi - lo) // 2
            yield from sort(lo, mid)
            yield from sort(mid + 1, hi)
            yield from merge(lo, hi, 1)

    return tuple(sort(0, n - 1))


def _compare_exchange(x, i, j):
    a, b = x[i], x[j]
    if b is None:
        return
    if a is None:
        x[i], x[j] = b, None
        return
    x[i], x[j] = jnp.maximum(a, b), jnp.minimum(a, b)


def _sort_desc(x):
    for i, j in _sorting_network(len(x)):
        _compare_exchange(x, i, j)
    return x


def _bitonic_sort_desc(x):
    n = len(x)
    stride = n // 2
    while stride >= 1:
        for i in range(n):
            if i & stride == 0:
                _compare_exchange(x, i, i + stride)
        stride //= 2
    return x


def _max_of(vals):
    vals = [v for v in vals if v is not None]
    while len(vals) > 1:
        vals = [jnp.maximum(vals[i], vals[i + 1]) for i in range(0, len(vals) - 1, 2)] + \
               ([vals[-1]] if len(vals) % 2 else [])
    return vals[0] if vals else None


def _merge_with_sublane_neighbour(x, runner_up, shift):
    n = len(x)

    def roll(a):
        return None if a is None else pltpu.roll(a, shift, axis=0)

    y = [roll(a) for a in x]
    hi, lo = [], []
    for i in range(n):
        a, b = x[i], y[n - 1 - i]
        if a is None or b is None:
            hi.append(b if a is None else a)
            lo.append(None)
        else:
            hi.append(jnp.maximum(a, b))
            lo.append(jnp.minimum(a, b))
    return _bitonic_sort_desc(hi), _max_of(lo + [runner_up, roll(runner_up)])


def _rank_over_sublanes_and_tiles(tiles):
    x, runner_up = _sort_desc(list(tiles)), None
    for shift in (4, 2, 1):
        x, runner_up = _merge_with_sublane_neighbour(x, runner_up, shift)
    return x + [runner_up]


def _peer_route_kernel(xn_ref, wq_ref, k1_ref, k2_ref,
                       a1_ref, p1_ref, s2_ref, p2_ref, st_scr):
    n_lt = st_scr.shape[1]
    pq = jnp.dot(xn_ref[...], wq_ref[...], preferred_element_type=F32).astype(BF16)
    nt = (((1,), (1,)), ((), ()))
    for h in range(PEER_HEADS):
        for half, k_ref in enumerate((k1_ref, k2_ref)):
            lo = h * PEER_QUERY_DIM + half * PEER_HALF
            sc = lax.dot_general(k_ref[h], pq[:, lo:lo + PEER_HALF], nt, preferred_element_type=F32)
            for lt in range(n_lt):
                st_scr[2 * h + half, lt] = sc[:, lt * LANES:(lt + 1) * LANES]

    n_tiles = PEER_N_KEYS // SUBLANES
    sub = lax.broadcasted_iota(jnp.int32, (SUBLANES, LANES), 0)

    def spread(vals):
        out = vals[-1]
        for r in range(len(vals) - 2, -1, -1):
            out = jnp.where(sub == r, vals[r], out)
        return out

    def per_slab(idx, carry):
        h = idx // n_lt
        lt = idx % n_lt
        s1 = st_scr[2 * h, lt]
        s2 = st_scr[2 * h + 1, lt]
        v1 = _rank_over_sublanes_and_tiles([s1[t * SUBLANES:(t + 1) * SUBLANES] for t in range(n_tiles)])
        v2 = _rank_over_sublanes_and_tiles([s2[t * SUBLANES:(t + 1) * SUBLANES] for t in range(n_tiles)])
        v2_lo, v2_hi = spread(v2[0:SUBLANES]), spread(v2[SUBLANES:2 * SUBLANES])
        v1_hi = spread(v1[SUBLANES:2 * SUBLANES])
        corner = jnp.where(sub == 0, v1[0] + v2[PEER_TOPK],
                           jnp.where(sub == 1, v1[PEER_TOPK] + v2[0], -jnp.inf))
        cands = [v1[0] + v2_lo, v1[0] + v2_hi] + [v1[i] + v2_lo for i in range(1, SUBLANES)]
        cands += [v1_hi + v2[0], corner]
        top = _rank_over_sublanes_and_tiles(cands + [None] * (n_tiles - len(cands)))
        cut = 0.5 * (top[PEER_TOPK - 1] + top[PEER_TOPK])
        z = jnp.ones((SUBLANES, LANES), F32)
        for r in range(1, PEER_TOPK):
            z = z + jnp.exp(top[r] - top[0])
        reps = (n_tiles, 1)
        a1_ref[h, lt] = jnp.tile(cut, reps) - s1
        s2_ref[h, lt] = s2
        p1_ref[h, lt] = jnp.exp(s1 - jnp.tile(v1[0], reps))
        p2_ref[h, lt] = jnp.exp(s2 - jnp.tile(v2[0], reps)) * jnp.tile(0.5 / z, reps)
        return carry

    lax.fori_loop(0, PEER_HEADS * n_lt, per_slab, 0)


def _peer_route(xn, wq, k1, k2, tm=512):
    s, d = xn.shape
    n_lt = tm // LANES
    stat = jax.ShapeDtypeStruct((PEER_HEADS, s // LANES, PEER_N_KEYS, LANES), F32)
    stat_spec = pl.BlockSpec((PEER_HEADS, n_lt, PEER_N_KEYS, LANES), lambda i: (0, i, 0, 0))
    return pl.pallas_call(
        _peer_route_kernel,
        grid=(s // tm,),
        in_specs=[pl.BlockSpec((tm, d), lambda i: (i, 0)),
                  pl.BlockSpec((d, PEER_HEADS * PEER_QUERY_DIM), lambda i: (0, 0)),
                  pl.BlockSpec((PEER_HEADS, PEER_N_KEYS, PEER_HALF), lambda i: (0, 0, 0)),
                  pl.BlockSpec((PEER_HEADS, PEER_N_KEYS, PEER_HALF), lambda i: (0, 0, 0))],
        out_specs=[stat_spec, stat_spec, stat_spec, stat_spec],
        out_shape=[stat, stat, stat, stat],
        scratch_shapes=[pltpu.VMEM((2 * PEER_HEADS, n_lt, PEER_N_KEYS, LANES), F32)],
        compiler_params=_params(("parallel",)),
        name="peer_route",
    )(xn, wq, k1, k2)


def _peer_dense_kernel(xn_ref, h_ref, a1_ref, p1_ref, s2_ref, p2_ref, u_ref, v_ref, o_ref):
    eb = pl.program_id(1)
    tt = xn_ref.shape[0]
    n_e1 = a1_ref.shape[2]
    e1_per_chain = n_e1 // N_CHAINS
    ce = e1_per_chain * PEER_N_KEYS

    @pl.when(eb == 0)
    def _():
        o_ref[...] = h_ref[...]

    xn = xn_ref[...]

    def pre_activations(c):
        return lax.dot_general(xn, u_ref[c * ce:(c + 1) * ce, :], (((1,), (1,)), ((), ())),
                               preferred_element_type=F32)

    st_next = pre_activations(0)
    for c in range(N_CHAINS):
        es = slice(c * ce, (c + 1) * ce)
        st = st_next
        if c + 1 < N_CHAINS:
            st_next = pre_activations(c + 1)
        row_blocks = []
        for lt in range(tt // LANES):
            col_blocks = []
            for jj in range(e1_per_chain):
                j = c * e1_per_chain + jj
                sb = st[lt * LANES:(lt + 1) * LANES, jj * PEER_N_KEYS:(jj + 1) * PEER_N_KEYS]
                w = jnp.zeros((PEER_N_KEYS, LANES), F32)
                for h in range(PEER_HEADS):
                    keep = s2_ref[h, lt] >= a1_ref[h, lt, j:j + 1, :]
                    w = w + p1_ref[h, lt, j:j + 1, :] * jnp.where(keep, p2_ref[h, lt], 0.0)
                col_blocks.append(_gelu_times_twice(sb, w.T))
            row_blocks.append(jnp.concatenate(col_blocks, axis=1))
        at = jnp.concatenate(row_blocks, axis=0).astype(BF16)
        o_ref[...] += jnp.dot(at, v_ref[es, :], preferred_element_type=F32)


def _peer_dense(xn, h1, a1, p1, s2, p2, u, v, tt=512, te=1024):
    s, d = xn.shape
    n_e1 = te // PEER_N_KEYS
    n_lt = tt // LANES
    return pl.pallas_call(
        _peer_dense_kernel,
        grid=(s // tt, PEER_N_EXPERTS // te),
        in_specs=[pl.BlockSpec((tt, d), lambda t, e: (t, 0)),
                  pl.BlockSpec((tt, d), lambda t, e: (t, 0)),
                  pl.BlockSpec((PEER_HEADS, n_lt, n_e1, LANES), lambda t, e: (0, t, e, 0)),
                  pl.BlockSpec((PEER_HEADS, n_lt, n_e1, LANES), lambda t, e: (0, t, e, 0)),
                  pl.BlockSpec((PEER_HEADS, n_lt, PEER_N_KEYS, LANES), lambda t, e: (0, t, 0, 0)),
                  pl.BlockSpec((PEER_HEADS, n_lt, PEER_N_KEYS, LANES), lambda t, e: (0, t, 0, 0)),
                  pl.BlockSpec((te, d), lambda t, e: (e, 0)),
                  pl.BlockSpec((te, d), lambda t, e: (e, 0))],
        out_specs=pl.BlockSpec((tt, d), lambda t, e: (t, 0)),
        out_shape=jax.ShapeDtypeStruct((s, d), F32),
        compiler_params=_params(("parallel", "arbitrary")),
        name="peer_dense",
    )(xn, h1, a1, p1, s2, p2, u, v)


def _rope_tables(seq_len):
    rows = seq_len // GRID_W
    row = jnp.repeat(jnp.arange(rows, dtype=F32), GRID_W)
    col = jnp.tile(jnp.arange(GRID_W, dtype=F32), rows)
    n_pairs = HEAD_DIM // 4
    inv_freq = ROPE_THETA ** (-jnp.arange(n_pairs, dtype=F32) / n_pairs)
    ang = jnp.concatenate([row[:, None] * inv_freq, col[:, None] * inv_freq], axis=-1)
    cos, sin = jnp.cos(ang), jnp.sin(ang)
    return jnp.concatenate([cos, cos], axis=-1), jnp.concatenate([-sin, sin], axis=-1)


def kernel(x, norm_mix, w_in, b_gate, q_norm, k_norm, sgu_norm, w_sgu, b_sgu, w_attn_proj,
           w_sgu_proj, w_out, norm_ffn, w_peer_q, peer_k1, peer_k2, peer_u, peer_v):
    b, s, d = x.shape
    assert b == 1 and norm_mix.shape[0] == 1
    l = 0
    xs = x.reshape(s, d)

    pair_perm = np.concatenate([np.arange(0, HEAD_DIM, 2), np.arange(1, HEAD_DIM, 2)])
    w = w_in[l]
    o_su = ATTN_WIDTH + 2 * KV_WIDTH
    o_sv = o_su + SGU_WIDTH
    o_g = o_sv + SGU_WIDTH
    n_qk = ATTN_WIDTH + KV_WIDTH
    w_qk = w[:, :n_qk].reshape(d, n_qk // HEAD_DIM, HEAD_DIM // 2, 2)
    w_qk = jnp.swapaxes(w_qk, 2, 3).reshape(d, n_qk)
    w_qkv = jnp.concatenate([w_qk, w[:, n_qk:n_qk + KV_WIDTH]], axis=1).astype(BF16)
    w_su = w[:, o_su:o_sv].astype(BF16)
    w_sv = w[:, o_sv:o_g].astype(BF16)
    w_ga = w[:, o_g:o_g + d].astype(BF16)
    w_gs = w[:, o_g + d:].astype(BF16)
    cos2, sin2 = _rope_tables(s)

    hn, q, k, v = _norm_qkv(xs, norm_mix[l].reshape(1, d), w_qkv, q_norm[l][pair_perm].reshape(1, HEAD_DIM),
                            k_norm[l][pair_perm].reshape(1, HEAD_DIM), cos2, sin2)
    attn = _attention(q, k, v)
    bias_s = jnp.broadcast_to(b_sgu[l][:, :, None], (SGU_GROUPS, SGU_CHUNK, 128))
    sgu = _sgu(hn, w_su, w_sv, sgu_norm[l].reshape(1, SGU_WIDTH), w_sgu[l].astype(BF16), bias_s)
    merged = _merge(hn, attn, sgu, w_ga, w_gs, b_gate[l][:d].reshape(1, d), b_gate[l][d:].reshape(1, d),
                    w_attn_proj[l].astype(BF16), w_sgu_proj[l].astype(BF16))
    h1, xn2 = _out_proj(xs, merged, w_out[l].astype(BF16), norm_ffn[l].reshape(1, d))
    a1, p1, s2, p2 = _peer_route(xn2, w_peer_q[l].astype(BF16), peer_k1[l].astype(BF16),
                                 peer_k2[l].astype(BF16))
    out = _peer_dense(xn2, h1, a1, p1, s2, p2, peer_u[l].astype(BF16), peer_v[l].astype(BF16))
    return out.reshape(b, s, d)
```
